```python
import math
import jax, jax.numpy as jnp
from jax import lax
import numpy as np

D_MODEL = 2048
BATCH = 4
SEQ = 8192
DEPTH = 1

GLA_HEADS = 4
GLA_HEAD_V = D_MODEL // 8
GLA_HEAD_K = GLA_HEAD_V // 2
GLA_DK = GLA_HEADS * GLA_HEAD_K
GLA_DV = GLA_HEADS * GLA_HEAD_V
GLA_GATE_RANK = 16
GLA_GATE_TAU = 16.0
GLA_CHUNK = 32

S5_GROUP_CH = 16
S5_WIDTH = D_MODEL // 4
S5_GROUPS = S5_WIDTH // S5_GROUP_CH
S5_STATE = 64
S5_DT_MIN = 1e-3
S5_DT_MAX = 1e-1

D_FF = ((8 * D_MODEL // 3 + 255) // 256) * 256

NORM_EPS = 1e-6
IN_WIDTHS = (GLA_DK, GLA_DK, GLA_DV, GLA_DV, GLA_GATE_RANK, S5_WIDTH, D_MODEL, D_MODEL)
D_IN = sum(IN_WIDTHS)

kernel_name = "hybrid_gla_s5_gated_block"


def _rmsnorm(x, g):
    xf = x.astype(jnp.float32)
    y = xf * lax.rsqrt(jnp.mean(xf * xf, axis=-1, keepdims=True) + NORM_EPS)
    return (y * g.astype(jnp.float32)).astype(x.dtype)


def _gla_branch(q, k, v, r, a_low, w_a2, b_a2, g_hn):
    f32 = jnp.float32
    bsz, seq, _ = q.shape
    n_chunks = seq // GLA_CHUNK
    z = (a_low @ w_a2 + b_a2).astype(f32)
    log_a = jax.nn.log_sigmoid(z) / GLA_GATE_TAU

    def chunks(t, hd):
        return t.astype(f32).reshape(bsz, n_chunks, GLA_CHUNK, GLA_HEADS, hd).transpose(0, 3, 1, 2, 4)

    qc = chunks(q, GLA_HEAD_K) * (GLA_HEAD_K ** -0.5)
    kc = chunks(k, GLA_HEAD_K)
    vc = chunks(v, GLA_HEAD_V)
    bc = jnp.cumsum(chunks(log_a, GLA_HEAD_K), axis=3)
    b_last = bc[..., -1:, :]
    q_dec = qc * jnp.exp(bc)
    k_intra = kc * jnp.exp(-bc)
    k_state = kc * jnp.exp(b_last - bc)

    causal = jnp.tril(jnp.ones((GLA_CHUNK, GLA_CHUNK), dtype=bool))
    scores = jnp.einsum('bhncd,bhnsd->bhncs', q_dec, k_intra)
    scores = jnp.where(causal, scores, 0.0)
    o_intra = jnp.einsum('bhncs,bhnsv->bhncv', scores, vc)

    def step(state, xs):
        qd, ks, vv, dl = xs
        o = jnp.einsum('bhcd,bhdv->bhcv', qd, state)
        state = dl[..., :, None] * state + jnp.einsum('bhcd,bhcv->bhdv', ks, vv)
        return state, o

    xs = (jnp.moveaxis(q_dec, 2, 0), jnp.moveaxis(k_state, 2, 0), jnp.moveaxis(vc, 2, 0),
          jnp.moveaxis(jnp.exp(b_last[..., 0, :]), 2, 0))
    s0 = jnp.zeros((bsz, GLA_HEADS, GLA_HEAD_K, GLA_HEAD_V), f32)
    _, o_inter = lax.scan(step, s0, xs)
    o = o_intra + jnp.moveaxis(o_inter, 0, 2)
    o = o.transpose(0, 2, 3, 1, 4).reshape(bsz, seq, GLA_HEADS, GLA_HEAD_V)
    o = o * lax.rsqrt(jnp.mean(o * o, axis=-1, keepdims=True) + NORM_EPS)
    o = o.reshape(bsz, seq, GLA_DV) * g_hn.astype(f32)
    return (jax.nn.silu(r.astype(f32)) * o).astype(q.dtype)


def _s5_branch(u, lam_re, lam_im, log_dt, b_re, b_im, c_re, c_im, d_skip, w_glu, b_glu):
    f32 = jnp.float32
    bsz, seq, _ = u.shape
    uf = u.astype(f32).reshape(bsz, seq, S5_GROUPS, S5_GROUP_CH)
    dt = jnp.exp(log_dt.astype(f32))[:, None]
    lr, li = lam_re.astype(f32), lam_im.astype(f32)
    mag = jnp.exp(lr * dt)
    abar_re = mag * jnp.cos(li * dt)
    abar_im = mag * jnp.sin(li * dt)
    den = lr * lr + li * li
    am1 = abar_re - 1.0
    f_re = ((am1 * lr + abar_im * li) / den)[..., None]
    f_im = ((abar_im * lr - am1 * li) / den)[..., None]
    br, bi = b_re.astype(f32), b_im.astype(f32)
    bbar_re = f_re * br - f_im * bi
    bbar_im = f_re * bi + f_im * br
    bu_re = jnp.einsum('btgc,gpc->tbgp', uf, bbar_re)
    bu_im = jnp.einsum('btgc,gpc->tbgp', uf, bbar_im)
    a_re = jnp.broadcast_to(abar_re, (seq, 1, S5_GROUPS, S5_STATE))
    a_im = jnp.broadcast_to(abar_im, (seq, 1, S5_GROUPS, S5_STATE))

    def combine(left, right):
        ar_l, ai_l, br_l, bi_l = left
        ar_r, ai_r, br_r, bi_r = right
        return (ar_r * ar_l - ai_r * ai_l,
                ar_r * ai_l + ai_r * ar_l,
                ar_r * br_l - ai_r * bi_l + br_r,
                ar_r * bi_l + ai_r * br_l + bi_r)

    _, _, xr, xi = lax.associative_scan(combine, (a_re, a_im, bu_re, bu_im), axis=0)
    y = (jnp.einsum('tbgp,gcp->btgc', xr, c_re.astype(f32))
         - jnp.einsum('tbgp,gcp->btgc', xi, c_im.astype(f32)))
    y = y.reshape(bsz, seq, S5_WIDTH) + d_skip.astype(f32) * u.astype(f32)
    h = jax.nn.gelu(y)
    out = h * jax.nn.sigmoid(h @ w_glu.astype(f32) + b_glu.astype(f32))
    return out.astype(u.dtype)


def setup_inputs(seed: int = 0) -> dict:
    key = jax.random.key(seed)
    ks = jax.random.split(key, 24)

    def nrm(k, shape, scale):
        return jax.random.normal(k, (DEPTH,) + shape, jnp.float32) * scale

    def gain(k, shape):
        return 1.0 + 0.02 * jax.random.normal(k, (DEPTH,) + shape, jnp.float32)

    n_idx = jnp.arange(S5_STATE, dtype=jnp.float32)
    lam_re = -0.5 + 0.01 * jax.random.normal(ks[5], (DEPTH, S5_GROUPS, S5_STATE), jnp.float32)
    lam_im = math.pi * n_idx + 0.01 * jax.random.normal(ks[6], (DEPTH, S5_GROUPS, S5_STATE), jnp.float32)
    log_dt = jax.random.uniform(ks[7], (DEPTH, S5_GROUPS), jnp.float32,
                                math.log(S5_DT_MIN), math.log(S5_DT_MAX))
    return {
        "x": jax.random.normal(ks[0], (BATCH, SEQ, D_MODEL), jnp.float32),
        "norm1_g": gain(ks[1], (D_MODEL,)),
        "w_in": nrm(ks[2], (D_MODEL, D_IN), D_MODEL ** -0.5),
        "w_a2": nrm(ks[3], (GLA_GATE_RANK, GLA_DK), GLA_GATE_RANK ** -0.5),
        "b_a2": nrm(ks[4], (GLA_DK,), 0.1),
        "gla_norm_g": gain(ks[8], (GLA_DV,)),
        "lam_re": lam_re,
        "lam_im": lam_im,
        "log_dt": log_dt,
        "s5_b_re": nrm(ks[9], (S5_GROUPS, S5_STATE, S5_GROUP_CH), (2 * S5_GROUP_CH) ** -0.5),
        "s5_b_im": nrm(ks[10], (S5_GROUPS, S5_STATE, S5_GROUP_CH), (2 * S5_GROUP_CH) ** -0.5),
        "s5_c_re": nrm(ks[11], (S5_GROUPS, S5_GROUP_CH, S5_STATE), (2 * S5_STATE) ** -0.5),
        "s5_c_im": nrm(ks[12], (S5_GROUPS, S5_GROUP_CH, S5_STATE), (2 * S5_STATE) ** -0.5),
        "s5_d": nrm(ks[13], (S5_WIDTH,), 1.0),
        "w_glu": nrm(ks[14], (S5_WIDTH, S5_WIDTH), S5_WIDTH ** -0.5),
        "b_glu": nrm(ks[15], (S5_WIDTH,), 0.02),
        "w_branch_a": nrm(ks[16], (GLA_DV, D_MODEL), GLA_DV ** -0.5),
        "w_branch_b": nrm(ks[17], (S5_WIDTH, D_MODEL), S5_WIDTH ** -0.5),
        "w_out": nrm(ks[18], (D_MODEL, D_MODEL), D_MODEL ** -0.5),
        "norm2_g": gain(ks[19], (D_MODEL,)),
        "w_ffn_in": nrm(ks[20], (D_MODEL, 2 * D_FF), D_MODEL ** -0.5),
        "w_ffn_out": nrm(ks[21], (D_FF, D_MODEL), D_FF ** -0.5),
        "final_norm_g": gain(ks[22], (D_MODEL,)),
    }


def reference(x, norm1_g, w_in, w_a2, b_a2, gla_norm_g, lam_re, lam_im, log_dt,
              s5_b_re, s5_b_im, s5_c_re, s5_c_im, s5_d, w_glu, b_glu,
              w_branch_a, w_branch_b, w_out, norm2_g, w_ffn_in, w_ffn_out, final_norm_g):
    splits = np.cumsum(IN_WIDTHS)[:-1].tolist()
    for l in range(DEPTH):
        h = _rmsnorm(x, norm1_g[l])
        proj = h @ w_in[l]
        q, k, v, r, a_low, u, gate_a, gate_b = jnp.split(proj, splits, axis=-1)
        o_a = _gla_branch(q, k, v, r, a_low, w_a2[l], b_a2[l], gla_norm_g[l])
        o_b = _s5_branch(u, lam_re[l], lam_im[l], log_dt[l], s5_b_re[l], s5_b_im[l],
                         s5_c_re[l], s5_c_im[l], s5_d[l], w_glu[l], b_glu[l])
        mix = (jax.nn.sigmoid(gate_a) * (o_a @ w_branch_a[l])
               + jax.nn.sigmoid(gate_b) * (o_b @ w_branch_b[l]))
        x = x + mix @ w_out[l]
        h = _rmsnorm(x, norm2_g[l])
        gate, up = jnp.split(h @ w_ffn_in[l], 2, axis=-1)
        x = x + (jax.nn.silu(gate) * up) @ w_ffn_out[l]
    return _rmsnorm(x, final_norm_g)
```

```python
import functools
import math

import jax
import jax.numpy as jnp
from jax import lax
from jax.experimental import pallas as pl
from jax.experimental.pallas import tpu as pltpu

F32 = jnp.float32
BF16 = jnp.bfloat16

NORM_EPS = 1e-6
GLA_HEADS = 4
GLA_CHUNK = 32
GLA_GATE_TAU = 16.0
LANES = 128
SUBLANES = 8
S5_STEPS = 8
S5_SLAB_GROUPS = 8
VMEM_LIMIT = 56 * 1024 * 1024


def _sigmoid(x):
    return 1.0 / (1.0 + jnp.exp(-x))


def _log_sigmoid(x):
    return jnp.minimum(x, 0.0) - jnp.log1p(jnp.exp(-jnp.abs(x)))


def _gelu_tanh(x):
    c = math.sqrt(2.0 / math.pi)
    return 0.5 * x * (1.0 + jnp.tanh(c * (x + 0.044715 * (x * x * x))))


def _dot(a, b):
    return jnp.dot(a, b, preferred_element_type=F32)


def _cparams(sem):
    return pltpu.CompilerParams(dimension_semantics=sem, vmem_limit_bytes=VMEM_LIMIT)


def _resident(shape):
    zeros = (0,) * len(shape)
    return pl.BlockSpec(shape, lambda *_: zeros, pipeline_mode=pl.Buffered(1))


def _in_proj_kernel(x_ref, g_ref, w_ref, wlow_ref, main_ref, u_ref, alow_ref, h_ref, *, n_main):
    j = pl.program_id(1)

    @pl.when(j == 0)
    def _():
        x = x_ref[...]
        ms = jnp.mean(x * x, axis=-1, keepdims=True)
        h_ref[...] = (x * lax.rsqrt(ms + NORM_EPS) * g_ref[...]).astype(BF16)

    acc = _dot(h_ref[...], w_ref[...])

    @pl.when(j < n_main)
    def _():
        main_ref[...] = acc.astype(BF16)

    @pl.when(j == n_main)
    def _():
        u_ref[...] = acc.astype(BF16)
        alow_ref[...] = _dot(h_ref[...], wlow_ref[...])


def _in_proj(x2, g, w_cat, w_low, *, tm, tn):
    n, d = x2.shape
    n_cols = w_cat.shape[1]
    n_main = n_cols // tn - 1
    kern = functools.partial(_in_proj_kernel, n_main=n_main)
    return pl.pallas_call(
        kern,
        grid=(n // tm, n_main + 1),
        in_specs=[
            pl.BlockSpec((tm, d), lambda i, j: (i, 0)),
            pl.BlockSpec((1, d), lambda i, j: (0, 0)),
            pl.BlockSpec((d, tn), lambda i, j: (0, j)),
            pl.BlockSpec((d, LANES), lambda i, j: (0, 0)),
        ],
        out_specs=[
            pl.BlockSpec((tm, tn), lambda i, j: (i, jnp.minimum(j, n_main - 1))),
            pl.BlockSpec((tm, tn), lambda i, j: (i, 0)),
            pl.BlockSpec((tm, LANES), lambda i, j: (i, 0)),
        ],
        out_shape=[
            jax.ShapeDtypeStruct((n, n_main * tn), BF16),
            jax.ShapeDtypeStruct((n, tn), BF16),
            jax.ShapeDtypeStruct((n, LANES), F32),
        ],
        scratch_shapes=[pltpu.VMEM((tm, d), BF16)],
        compiler_params=_cparams(("parallel", "arbitrary")),
        name="in_proj",
    )(x2, g, w_cat, w_low)


def _gla_kernel(q_ref, k_ref, v_ref, r_ref, alow_ref, wa2_ref, ba2_ref, ghn_ref, tri_ref,
                o_ref, s_ref, *, n_sub, dk, dv):
    c = GLA_CHUNK

    @pl.when(pl.program_id(1) == 0)
    def _():
        s_ref[...] = jnp.zeros_like(s_ref)

    z = _dot(alow_ref[...], wa2_ref[...]) + ba2_ref[...]
    log_a = _log_sigmoid(z) * (1.0 / GLA_GATE_TAU)
    la_hi = log_a.astype(BF16)
    la_lo = (log_a - la_hi.astype(F32)).astype(BF16)
    tri = tri_ref[...]
    bc_all = _dot(tri, la_hi) + _dot(tri, la_lo)

    row = lax.broadcasted_iota(jnp.int32, (c, c), 0)
    col = lax.broadcasted_iota(jnp.int32, (c, c), 1)
    causal = row >= col
    scale = dk ** -0.5

    for ci in range(n_sub):
        rows = slice(ci * c, (ci + 1) * c)
        for h in range(GLA_HEADS):
            ks_ = slice(h * dk, (h + 1) * dk)
            vs_ = slice(h * dv, (h + 1) * dv)
            b = bc_all[rows, ks_]
            bl = b[c - 1:c, :]
            q = q_ref[rows, ks_].astype(F32) * scale
            k = k_ref[rows, ks_].astype(F32)
            v = v_ref[rows, vs_]
            q_dec = (q * jnp.exp(b)).astype(BF16)
            k_intra = (k * jnp.exp(-b)).astype(BF16)
            k_state = (k * jnp.exp(bl - b)).astype(BF16)
            decay = jnp.exp(bl)

            scores = lax.dot_general(q_dec, k_intra, (((1,), (1,)), ((), ())),
                                     preferred_element_type=F32)
            scores = jnp.where(causal, scores, 0.0).astype(BF16)
            state = s_ref[h]
            o = _dot(scores, v) + _dot(q_dec, state.astype(BF16))

            upd = lax.dot_general(k_state, v, (((0,), (0,)), ((), ())),
                                  preferred_element_type=F32)
            decay_col = jnp.transpose(jnp.broadcast_to(decay, (dk, dk)))
            if dv > dk:
                decay_col = jnp.concatenate([decay_col] * (dv // dk), axis=1)
            s_ref[h] = decay_col * state + upd

            o = o * lax.rsqrt(jnp.mean(o * o, axis=-1, keepdims=True) + NORM_EPS)
            o = o * ghn_ref[:, vs_]
            r = r_ref[rows, vs_].astype(F32)
            o_ref[rows, vs_] = (r * _sigmoid(r) * o).astype(BF16)


def _gla(main3, alow3, wa2p, ba2, ghn, tri, *, tc, col_q, col_k, col_v, col_r, dk, dv):
    b, t, _ = main3.shape
    hk, hv = GLA_HEADS * dk, GLA_HEADS * dv
    kern = functools.partial(_gla_kernel, n_sub=tc // GLA_CHUNK, dk=dk, dv=dv)
    return pl.pallas_call(
        kern,
        grid=(b, t // tc),
        in_specs=[
            pl.BlockSpec((None, tc, hk), lambda i, j: (i, j, col_q // hk)),
            pl.BlockSpec((None, tc, hk), lambda i, j: (i, j, col_k // hk)),
            pl.BlockSpec((None, tc, hv), lambda i, j: (i, j, col_v // hv)),
            pl.BlockSpec((None, tc, hv), lambda i, j: (i, j, col_r // hv)),
            pl.BlockSpec((None, tc, LANES), lambda i, j: (i, j, 0)),
            pl.BlockSpec((LANES, hk), lambda i, j: (0, 0)),
            pl.BlockSpec((1, hk), lambda i, j: (0, 0)),
            pl.BlockSpec((1, hv), lambda i, j: (0, 0)),
            pl.BlockSpec((tc, tc), lambda i, j: (0, 0)),
        ],
        out_specs=pl.BlockSpec((None, tc, hv), lambda i, j: (i, j, 0)),
        out_shape=jax.ShapeDtypeStruct((b, t, hv), BF16),
        scratch_shapes=[pltpu.VMEM((GLA_HEADS, dk, dv), F32)],
        compiler_params=_cparams(("parallel", "arbitrary")),
        name="gla",
    )(main3, main3, main3, main3, alow3, wa2p, ba2, ghn, tri)


def _cmul(ar, ai, br, bi):
    return ar * br - ai * bi, ar * bi + ai * br


def _s5_tables_kernel(lr_r, li_r, ldt_r, lr_c, li_c, ldt_c, bre, bim, cre, cim,
                      m_ref, wst_ref, wout_ref, pw_ref, *, gc, p):
    L = S5_STEPS
    sw = S5_SLAB_GROUPS * gc
    sp = S5_SLAB_GROUPS * p

    def discretise(lr, li, ldt):
        dt = jnp.exp(ldt)
        mag = jnp.exp(lr * dt)
        return mag * jnp.cos(li * dt), mag * jnp.sin(li * dt)

    lr, li = lr_r[...], li_r[...]
    ar, ai = discretise(lr, li, ldt_r[...])
    den = lr * lr + li * li
    am1 = ar - 1.0
    f_re = (am1 * lr + ai * li) / den
    f_im = (ai * lr - am1 * li) / den

    rg = lax.broadcasted_iota(jnp.int32, (sw, sp), 0) // gc
    lg = lax.broadcasted_iota(jnp.int32, (sw, sp), 1) // p
    bmask = rg == lg
    br, bi = bre[...], bim[...]
    bbr = jnp.where(bmask, f_re * br - f_im * bi, 0.0)
    bbi = jnp.where(bmask, f_re * bi + f_im * br, 0.0)

    rg2 = lax.broadcasted_iota(jnp.int32, (sp, sw), 0) // p
    lg2 = lax.broadcasted_iota(jnp.int32, (sp, sw), 1) // gc
    cmask = rg2 == lg2
    cr = jnp.where(cmask, cre[...], 0.0)
    ci = jnp.where(cmask, cim[...], 0.0)

    pr, pi = [jnp.ones_like(ar)], [jnp.zeros_like(ai)]
    for _ in range(L):
        nr, ni = _cmul(pr[-1], pi[-1], ar, ai)
        pr.append(nr)
        pi.append(ni)

    for j in range(L):
        xr, xi = _cmul(bbr, bbi, pr[L - 1 - j], pi[L - 1 - j])
        wst_ref[j * sw:(j + 1) * sw, 0:sp] = xr.astype(BF16)
        wst_ref[j * sw:(j + 1) * sw, sp:2 * sp] = xi.astype(BF16)

    kern = []
    for k in range(L):
        xr, xi = _cmul(bbr, bbi, pr[k], pi[k])
        kk = (jnp.dot(xr, cr, preferred_element_type=F32, precision=lax.Precision.HIGHEST)
              - jnp.dot(xi, ci, preferred_element_type=F32, precision=lax.Precision.HIGHEST))
        kern.append(kk.astype(BF16))
    zero = jnp.zeros((sw, sw), BF16)
    for j in range(L):
        for i in range(L):
            m_ref[j * sw:(j + 1) * sw, i * sw:(i + 1) * sw] = kern[i - j] if i >= j else zero

    acr, aci = discretise(lr_c[...], li_c[...], ldt_c[...])
    qr, qi = acr, aci
    for i in range(L):
        zr, zi = _cmul(cr, ci, qr, qi)
        wout_ref[0:sp, i * sw:(i + 1) * sw] = zr.astype(BF16)
        wout_ref[sp:2 * sp, i * sw:(i + 1) * sw] = (-zi).astype(BF16)
        qr, qi = _cmul(qr, qi, acr, aci)

    a8r, a8i = pr[L], pi[L]
    tr, ti = a8r, a8i
    for i in range(SUBLANES):
        pw_ref[i:i + 1, 0:sp] = tr
        pw_ref[i:i + 1, sp:2 * sp] = ti
        tr, ti = _cmul(tr, ti, a8r, a8i)


def _s5_tables(lam_re, lam_im, log_dt, b_re, b_im, c_re, c_im):
    g, p = lam_re.shape
    gc = b_re.shape[2]
    ns = g // S5_SLAB_GROUPS
    sw, sp = S5_SLAB_GROUPS * gc, S5_SLAB_GROUPS * p
    lw = S5_STEPS * sw

    def rows(a):
        return a.reshape(ns, 1, sp)

    def cols(a):
        return a.reshape(ns, sp, 1)

    ldt = jnp.broadcast_to(log_dt[:, None], (g, p))

    def b_layout(b):
        bt = jnp.transpose(b, (0, 2, 1)).reshape(ns, sw, p)
        return jnp.tile(bt, (1, 1, S5_SLAB_GROUPS))

    def c_layout(c):
        ct = jnp.transpose(c.reshape(ns, S5_SLAB_GROUPS, gc, p), (0, 3, 1, 2)).reshape(ns, p, sw)
        return jnp.tile(ct, (1, S5_SLAB_GROUPS, 1))

    row_spec = pl.BlockSpec((None, 1, sp), lambda s: (s, 0, 0))
    col_spec = pl.BlockSpec((None, sp, 1), lambda s: (s, 0, 0))
    b_spec = pl.BlockSpec((None, sw, sp), lambda s: (s, 0, 0))
    c_spec = pl.BlockSpec((None, sp, sw), lambda s: (s, 0, 0))
    kern = functools.partial(_s5_tables_kernel, gc=gc, p=p)
    return pl.pallas_call(
        kern,
        grid=(ns,),
        in_specs=[row_spec, row_spec, row_spec, col_spec, col_spec, col_spec,
                  b_spec, b_spec, c_spec, c_spec],
        out_specs=[
            pl.BlockSpec((None, lw, lw), lambda s: (s, 0, 0)),
            pl.BlockSpec((None, lw, 2 * sp), lambda s: (s, 0, 0)),
            pl.BlockSpec((None, 2 * sp, lw), lambda s: (s, 0, 0)),
            pl.BlockSpec((None, SUBLANES, 2 * sp), lambda s: (s, 0, 0)),
        ],
        out_shape=[
            jax.ShapeDtypeStruct((ns, lw, lw), BF16),
            jax.ShapeDtypeStruct((ns, lw, 2 * sp), BF16),
            jax.ShapeDtypeStruct((ns, 2 * sp, lw), BF16),
            jax.ShapeDtypeStruct((ns, SUBLANES, 2 * sp), F32),
        ],
        compiler_params=_cparams(("parallel",)),
        name="s5_tables",
    )(rows(lam_re), rows(lam_im), rows(ldt), cols(lam_re), cols(lam_im), cols(ldt),
      b_layout(b_re), b_layout(b_im), c_layout(c_re), c_layout(c_im))


def _s5_kernel(u_ref, m_ref, wst_ref, wout_ref, pw_ref, d_ref, wglu_ref, bglu_ref,
               o_ref, carry_ref, sloc_ref, sprev_ref, y_ref, *, n_slabs, width):
    L = S5_STEPS
    sw = LANES
    rb = u_ref.shape[0]
    sp2 = sloc_ref.shape[1]
    sp = sp2 // 2

    @pl.when(pl.program_id(1) == 0)
    def _():
        carry_ref[...] = jnp.zeros_like(carry_ref)

    row_id = lax.broadcasted_iota(jnp.int32, (SUBLANES, sp2), 0)

    for s in range(n_slabs):
        u_slab = jnp.concatenate(
            [u_ref[:, j * width + s * sw: j * width + (s + 1) * sw] for j in range(L)], axis=1)
        sloc_ref[...] = _dot(u_slab, wst_ref[s])

        pw = pw_ref[s]
        pwr, pwi = pw[:, :sp], pw[:, sp:]

        def group(gi, carry, pwr=pwr, pwi=pwi):
            base = pl.multiple_of(gi * SUBLANES, SUBLANES)
            x = sloc_ref[pl.ds(base, SUBLANES), :]
            for sh in (1, 2, 4):
                sft = jnp.where(row_id >= sh, pltpu.roll(x, sh, axis=0), 0.0)
                ar = pwr[sh - 1:sh, :]
                ai = pwi[sh - 1:sh, :]
                tr, ti = _cmul(sft[:, :sp], sft[:, sp:], ar, ai)
                x = x + jnp.concatenate([tr, ti], axis=1)
            cr, ci = _cmul(carry[:, :sp], carry[:, sp:], pwr, pwi)
            incl = x + jnp.concatenate([cr, ci], axis=1)
            excl = jnp.where(row_id == 0, carry, pltpu.roll(incl, 1, axis=0))
            sprev_ref[pl.ds(base, SUBLANES), :] = excl
            return incl[SUBLANES - 1:SUBLANES, :]

        carry = lax.fori_loop(0, rb // SUBLANES, group, carry_ref[s])
        carry_ref[s] = carry

        y = _dot(u_slab, m_ref[s]) + _dot(sprev_ref[...].astype(BF16), wout_ref[s])
        for i in range(L):
            y_ref[:, i * width + s * sw: i * width + (s + 1) * sw] = y[:, i * sw:(i + 1) * sw]

    for i in range(L):
        cols = slice(i * width, (i + 1) * width)
        yi = y_ref[:, cols] + d_ref[...] * u_ref[:, cols].astype(F32)
        hh = _gelu_tanh(yi)
        gate = _sigmoid(_dot(hh.astype(BF16), wglu_ref[...]) + bglu_ref[...])
        o_ref[:, cols] = (hh * gate).astype(BF16)


def _s5(u3, m, wst, wout, pw, d_skip, wglu, bglu, *, rb):
    b, rows, lw = u3.shape
    ns = m.shape[0]
    width = lw // S5_STEPS
    sp2 = wst.shape[2]
    kern = functools.partial(_s5_kernel, n_slabs=ns, width=width)
    return pl.pallas_call(
        kern,
        grid=(b, rows // rb),
        in_specs=[
            pl.BlockSpec((None, rb, lw), lambda i, j: (i, j, 0)),
            _resident(m.shape), _resident(wst.shape), _resident(wout.shape), _resident(pw.shape),
            _resident(d_skip.shape), _resident(wglu.shape), _resident(bglu.shape),
        ],
        out_specs=pl.BlockSpec((None, rb, lw), lambda i, j: (i, j, 0)),
        out_shape=jax.ShapeDtypeStruct((b, rows, lw), BF16),
        scratch_shapes=[
            pltpu.VMEM((ns, 1, sp2), F32),
            pltpu.VMEM((rb, sp2), F32),
            pltpu.VMEM((rb, sp2), F32),
            pltpu.VMEM((rb, lw), F32),
        ],
        compiler_params=_cparams(("parallel", "arbitrary")),
        name="s5",
    )(u3, m, wst, wout, pw, d_skip, wglu, bglu)


def _mix_kernel(x_ref, ga_ref, gb_ref, oa_ref, ob_ref, wa_ref, wb_ref, wo_ref, o_ref):
    a = _dot(oa_ref[...], wa_ref[...])
    b = _dot(ob_ref[...], wb_ref[...])
    mix = _sigmoid(ga_ref[...].astype(F32)) * a + _sigmoid(gb_ref[...].astype(F32)) * b
    o_ref[...] = x_ref[...] + _dot(mix.astype(BF16), wo_ref[...])


def _mix(x2, main, o_a, o_b, w_a, w_b, w_o, *, tm):
    n, d = x2.shape
    return pl.pallas_call(
        _mix_kernel,
        grid=(n // tm,),
        in_specs=[
            pl.BlockSpec((tm, d), lambda i: (i, 0)),
            pl.BlockSpec((tm, d), lambda i: (i, 0)),
            pl.BlockSpec((tm, d), lambda i: (i, 1)),
            pl.BlockSpec((tm, o_a.shape[1]), lambda i: (i, 0)),
            pl.BlockSpec((tm, o_b.shape[1]), lambda i: (i, 0)),
            _resident(w_a.shape), _resident(w_b.shape), _resident(w_o.shape),
        ],
        out_specs=pl.BlockSpec((tm, d), lambda i: (i, 0)),
        out_shape=jax.ShapeDtypeStruct((n, d), F32),
        compiler_params=_cparams(("parallel",)),
        name="mix",
    )(x2, main, main, o_a, o_b, w_a, w_b, w_o)


def _ffn_kernel(x_ref, g2_ref, wg_ref, wu_ref, wd_ref, gf_ref, o_ref, h_ref, acc_ref):
    j = pl.program_id(1)

    @pl.when(j == 0)
    def _():
        x = x_ref[...]
        ms = jnp.mean(x * x, axis=-1, keepdims=True)
        h_ref[...] = (x * lax.rsqrt(ms + NORM_EPS) * g2_ref[...]).astype(BF16)
        acc_ref[...] = jnp.zeros_like(acc_ref)

    h = h_ref[...]
    gate = _dot(h, wg_ref[...])
    up = _dot(h, wu_ref[...])
    act = (gate * _sigmoid(gate) * up).astype(BF16)
    acc_ref[...] += _dot(act, wd_ref[...])

    @pl.when(j == pl.num_programs(1) - 1)
    def _():
        y = x_ref[...] + acc_ref[...]
        ms = jnp.mean(y * y, axis=-1, keepdims=True)
        o_ref[...] = y * lax.rsqrt(ms + NORM_EPS) * gf_ref[...]


def _ffn(x1, g2, w_in, w_out, gf, *, tm, tf):
    n, d = x1.shape
    f = w_out.shape[0]
    nf = f // tf
    return pl.pallas_call(
        _ffn_kernel,
        grid=(n // tm, nf),
        in_specs=[
            pl.BlockSpec((tm, d), lambda i, j: (i, 0)),
            pl.BlockSpec((1, d), lambda i, j: (0, 0)),
            pl.BlockSpec((d, tf), lambda i, j: (0, j)),
            pl.BlockSpec((d, tf), lambda i, j: (0, j + nf)),
            pl.BlockSpec((tf, d), lambda i, j: (j, 0)),
            pl.BlockSpec((1, d), lambda i, j: (0, 0)),
        ],
        out_specs=pl.BlockSpec((tm, d), lambda i, j: (i, 0)),
        out_shape=jax.ShapeDtypeStruct((n, d), F32),
        scratch_shapes=[pltpu.VMEM((tm, d), BF16), pltpu.VMEM((tm, d), F32)],
        compiler_params=_cparams(("parallel", "arbitrary")),
        name="ffn",
    )(x1, g2, w_in, w_in, w_out, gf)


def _tile(n, pref):
    t = min(n, pref)
    assert n % t == 0, (n, t)
    return t


def _layer(x2, bsz, seq, norm1_g, w_in, w_a2, b_a2, gla_norm_g, lam_re, lam_im, log_dt,
           s5_b_re, s5_b_im, s5_c_re, s5_c_im, s5_d, w_glu, b_glu,
           w_branch_a, w_branch_b, w_out, norm2_g, w_ffn_in, w_ffn_out, final_g):
    n, d = x2.shape
    hk = w_a2.shape[1]
    hv = gla_norm_g.shape[0]
    rank = w_a2.shape[0]
    sw = w_glu.shape[0]
    dk, dv = hk // GLA_HEADS, hv // GLA_HEADS

    o_q, o_k, o_v, o_r = 0, hk, 2 * hk, 2 * hk + hv
    o_al = 2 * hk + 2 * hv
    o_u = o_al + rank
    o_ga = o_u + sw
    o_gb = o_ga + d
    wi = w_in.astype(BF16)
    w_cat = jnp.concatenate(
        [wi[:, o_ga:o_ga + d], wi[:, o_gb:o_gb + d], wi[:, o_v:o_v + hv], wi[:, o_r:o_r + hv],
         wi[:, o_q:o_q + hk], wi[:, o_k:o_k + hk], wi[:, o_u:o_u + sw]], axis=1)
    w_low = jnp.pad(wi[:, o_al:o_al + rank], ((0, 0), (0, LANES - rank)))
    col_v, col_r, col_q, col_k = 2 * d, 2 * d + hv, 2 * d + 2 * hv, 2 * d + 2 * hv + hk

    main, u, alow = _in_proj(x2, norm1_g.reshape(1, d), w_cat, w_low, tm=_tile(n, 1024), tn=sw)

    tc = _tile(seq, 256)
    blk = jnp.arange(tc) // GLA_CHUNK
    tri = ((blk[:, None] == blk[None, :]) & (jnp.arange(tc)[:, None] >= jnp.arange(tc)[None, :]))
    wa2p = jnp.pad(w_a2, ((0, LANES - rank), (0, 0)))
    o_a = _gla(main.reshape(bsz, seq, -1), alow.reshape(bsz, seq, LANES), wa2p,
               b_a2.reshape(1, hk), gla_norm_g.reshape(1, hv), tri.astype(BF16),
               tc=tc, col_q=col_q, col_k=col_k, col_v=col_v, col_r=col_r, dk=dk, dv=dv)

    m, wst, wout, pw = _s5_tables(lam_re, lam_im, log_dt, s5_b_re, s5_b_im, s5_c_re, s5_c_im)
    rows = seq // S5_STEPS
    o_b = _s5(u.reshape(bsz, rows, S5_STEPS * sw), m, wst, wout, pw.reshape(pw.shape[0], SUBLANES, -1),
              s5_d.reshape(1, sw), w_glu.astype(BF16), b_glu.reshape(1, sw), rb=_tile(rows, 256))

    x1 = _mix(x2, main, o_a.reshape(n, hv), o_b.reshape(n, sw), w_branch_a.astype(BF16),
              w_branch_b.astype(BF16), w_out.astype(BF16), tm=_tile(n, 512))
    return _ffn(x1, norm2_g.reshape(1, d), w_ffn_in.astype(BF16), w_ffn_out.astype(BF16),
                final_g.reshape(1, d), tm=_tile(n, 512), tf=512)


def kernel(x, norm1_g, w_in, w_a2, b_a2, gla_norm_g, lam_re, lam_im, log_dt, s5_b_re, s5_b_im,
           s5_c_re, s5_c_im, s5_d, w_glu, b_glu, w_branch_a, w_branch_b, w_out, norm2_g,
           w_ffn_in, w_ffn_out, final_norm_g):
    bsz, seq, d = x.shape
    assert norm1_g.shape[0] == 1, "single-layer block"
    out = _layer(x.reshape(bsz * seq, d), bsz, seq, norm1_g[0], w_in[0], w_a2[0], b_a2[0],
                 gla_norm_g[0], lam_re[0], lam_im[0], log_dt[0], s5_b_re[0], s5_b_im[0],
                 s5_c_re[0], s5_c_im[0], s5_d[0], w_glu[0], b_glu[0], w_branch_a[0],
                 w_branch_b[0], w_out[0], norm2_g[0], w_ffn_in[0], w_ffn_out[0], final_norm_g[0])
    return out.reshape(bsz, seq, d)
```

```python
import functools
import math

import jax
import jax.numpy as jnp
from jax import lax
from jax.experimental import pallas as pl
from jax.experimental.pallas import tpu as pltpu

F32 = jnp.float32
BF16 = jnp.bfloat16

NORM_EPS = 1e-6
GLA_HEADS = 4
GLA_CHUNK = 32
GLA_GATE_TAU = 16.0
LANES = 128
SUBLANES = 8
S5_STEPS = 8
S5_SLAB_GROUPS = 8
VMEM_LIMIT = 56 * 1024 * 1024


def _sigmoid(x):
    return 1.0 / (1.0 + jnp.exp(-x))


def _log_sigmoid(x):
    return jnp.minimum(x, 0.0) - jnp.log1p(jnp.exp(-jnp.abs(x)))


def _gelu_tanh(x):
    c = math.sqrt(2.0 / math.pi)
    return 0.5 * x * (1.0 + jnp.tanh(c * (x + 0.044715 * (x * x * x))))


def _dot(a, b):
    return jnp.dot(a, b, preferred_element_type=F32)


def _cparams(sem):
    return pltpu.CompilerParams(dimension_semantics=sem, vmem_limit_bytes=VMEM_LIMIT)


def _resident(shape):
    zeros = (0,) * len(shape)
    return pl.BlockSpec(shape, lambda *_: zeros, pipeline_mode=pl.Buffered(1))


def _in_proj_kernel(x_ref, g_ref, w_ref, wu_ref, wlow_ref, main_ref, u_ref, alow_ref, h_ref, uacc_ref,
                    *, n_main):
    j = pl.program_id(1)

    @pl.when(j == 0)
    def _():
        x = x_ref[...]
        ms = jnp.mean(x * x, axis=-1, keepdims=True)
        h_ref[...] = (x * lax.rsqrt(ms + NORM_EPS) * g_ref[...]).astype(BF16)

    @pl.when(j < n_main)
    def _():
        main_ref[...] = _dot(h_ref[...], w_ref[...]).astype(BF16)

    @pl.when(j == n_main)
    def _():
        h = h_ref[...]
        alow_ref[...] = _dot(h, wlow_ref[...])
        uacc = _dot(h, wu_ref[...])
        rows, width = u_ref.shape[0], wu_ref.shape[1]
        for c in range(width // LANES):
            uacc_ref[c] = uacc[:, c * LANES:(c + 1) * LANES]
        for s in range(S5_STEPS):
            for c in range(width // LANES):
                lo = s * width + c * LANES
                u_ref[:, lo:lo + LANES] = uacc_ref[c, pl.ds(s, rows, stride=S5_STEPS), :].astype(BF16)


def _in_proj(x2, g, w_main, w_u, w_low, *, tm, tn):
    n, d = x2.shape
    n_main = w_main.shape[1] // tn
    sw = w_u.shape[1]
    kern = functools.partial(_in_proj_kernel, n_main=n_main)
    return pl.pallas_call(
        kern,
        grid=(n // tm, n_main + 1),
        in_specs=[
            pl.BlockSpec((tm, d), lambda i, j: (i, 0)),
            pl.BlockSpec((1, d), lambda i, j: (0, 0)),
            pl.BlockSpec((d, tn), lambda i, j: (0, jnp.minimum(j, n_main - 1))),
            _resident(w_u.shape),
            _resident(w_low.shape),
        ],
        out_specs=[
            pl.BlockSpec((tm, tn), lambda i, j: (i, jnp.minimum(j, n_main - 1))),
            pl.BlockSpec((tm // S5_STEPS, S5_STEPS * sw), lambda i, j: (i, 0)),
            pl.BlockSpec((tm, LANES), lambda i, j: (i, 0)),
        ],
        out_shape=[
            jax.ShapeDtypeStruct((n, n_main * tn), BF16),
            jax.ShapeDtypeStruct((n // S5_STEPS, S5_STEPS * sw), BF16),
            jax.ShapeDtypeStruct((n, LANES), F32),
        ],
        scratch_shapes=[pltpu.VMEM((tm, d), BF16), pltpu.VMEM((sw // LANES, tm, LANES), F32)],
        compiler_params=_cparams(("parallel", "arbitrary")),
        name="in_proj",
    )(x2, g, w_main, w_u, w_low)


def _gla_kernel(q_ref, k_ref, v_ref, r_ref, alow_ref, wa2_ref, ba2_ref, ghn_ref, tri_ref,
                o_ref, s_ref, *, n_sub, dk, dv):
    c = GLA_CHUNK

    @pl.when(pl.program_id(1) == 0)
    def _():
        s_ref[...] = jnp.zeros_like(s_ref)

    z = _dot(alow_ref[...], wa2_ref[...]) + ba2_ref[...]
    log_a = _log_sigmoid(z) * (1.0 / GLA_GATE_TAU)
    la_hi = log_a.astype(BF16)
    la_lo = (log_a - la_hi.astype(F32)).astype(BF16)
    tri = tri_ref[...]
    bc_all = _dot(tri, la_hi) + _dot(tri, la_lo)

    row = lax.broadcasted_iota(jnp.int32, (c, c), 0)
    col = lax.broadcasted_iota(jnp.int32, (c, c), 1)
    causal = row >= col
    scale = dk ** -0.5

    for ci in range(n_sub):
        rows = slice(ci * c, (ci + 1) * c)
        for h in range(GLA_HEADS):
            ks_ = slice(h * dk, (h + 1) * dk)
            vs_ = slice(h * dv, (h + 1) * dv)
            b = bc_all[rows, ks_]
            bl = b[c - 1:c, :]
            q = q_ref[rows, ks_].astype(F32) * scale
            k = k_ref[rows, ks_].astype(F32)
            v = v_ref[rows, vs_]
            q_dec = (q * jnp.exp(b)).astype(BF16)
            k_intra = (k * jnp.exp(-b)).astype(BF16)
            k_state = (k * jnp.exp(bl - b)).astype(BF16)
            decay = jnp.exp(bl)

            scores = lax.dot_general(q_dec, k_intra, (((1,), (1,)), ((), ())),
                                     preferred_element_type=F32)
            scores = jnp.where(causal, scores, 0.0).astype(BF16)
            state = s_ref[h]
            o = _dot(scores, v) + _dot(q_dec, state.astype(BF16))

            upd = lax.dot_general(k_state, v, (((0,), (0,)), ((), ())),
                                  preferred_element_type=F32)
            decay_col = jnp.transpose(jnp.broadcast_to(decay, (dk, dk)))
            if dv > dk:
                decay_col = jnp.concatenate([decay_col] * (dv // dk), axis=1)
            s_ref[h] = decay_col * state + upd

            o = o * lax.rsqrt(jnp.mean(o * o, axis=-1, keepdims=True) + NORM_EPS)
            o = o * ghn_ref[:, vs_]
            r = r_ref[rows, vs_].astype(F32)
            o_ref[rows, vs_] = (r * _sigmoid(r) * o).astype(BF16)


def _gla(main3, alow3, wa2p, ba2, ghn, tri, *, tc, col_q, col_k, col_v, col_r, dk, dv):
    b, t, _ = main3.shape
    hk, hv = GLA_HEADS * dk, GLA_HEADS * dv
    kern = functools.partial(_gla_kernel, n_sub=tc // GLA_CHUNK, dk=dk, dv=dv)
    return pl.pallas_call(
        kern,
        grid=(b, t // tc),
        in_specs=[
            pl.BlockSpec((None, tc, hk), lambda i, j: (i, j, col_q // hk)),
            pl.BlockSpec((None, tc, hk), lambda i, j: (i, j, col_k // hk)),
            pl.BlockSpec((None, tc, hv), lambda i, j: (i, j, col_v // hv)),
            pl.BlockSpec((None, tc, hv), lambda i, j: (i, j, col_r // hv)),
            pl.BlockSpec((None, tc, LANES), lambda i, j: (i, j, 0)),
            pl.BlockSpec((LANES, hk), lambda i, j: (0, 0)),
            pl.BlockSpec((1, hk), lambda i, j: (0, 0)),
            pl.BlockSpec((1, hv), lambda i, j: (0, 0)),
            pl.BlockSpec((tc, tc), lambda i, j: (0, 0)),
        ],
        out_specs=pl.BlockSpec((None, tc, hv), lambda i, j: (i, j, 0)),
        out_shape=jax.ShapeDtypeStruct((b, t, hv), BF16),
        scratch_shapes=[pltpu.VMEM((GLA_HEADS, dk, dv), F32)],
        compiler_params=_cparams(("parallel", "arbitrary")),
        name="gla",
    )(main3, main3, main3, main3, alow3, wa2p, ba2, ghn, tri)


def _cmul(ar, ai, br, bi):
    return ar * br - ai * bi, ar * bi + ai * br


def _s5_tables_kernel(lr_r, li_r, ldt_r, lr_c, li_c, ldt_c, bre, bim, cre, cim,
                      m_ref, wst_ref, wout_ref, pw_ref, *, gc, p):
    L = S5_STEPS
    sw = S5_SLAB_GROUPS * gc
    sp = S5_SLAB_GROUPS * p

    def discretise(lr, li, ldt):
        dt = jnp.exp(ldt)
        mag = jnp.exp(lr * dt)
        return mag * jnp.cos(li * dt), mag * jnp.sin(li * dt)

    lr, li = lr_r[...], li_r[...]
    ar, ai = discretise(lr, li, ldt_r[...])
    den = lr * lr + li * li
    am1 = ar - 1.0
    f_re = (am1 * lr + ai * li) / den
    f_im = (ai * lr - am1 * li) / den

    rg = lax.broadcasted_iota(jnp.int32, (sw, sp), 0) // gc
    lg = lax.broadcasted_iota(jnp.int32, (sw, sp), 1) // p
    bmask = rg == lg
    br, bi = bre[...], bim[...]
    bbr = jnp.where(bmask, f_re * br - f_im * bi, 0.0)
    bbi = jnp.where(bmask, f_re * bi + f_im * br, 0.0)

    rg2 = lax.broadcasted_iota(jnp.int32, (sp, sw), 0) // p
    lg2 = lax.broadcasted_iota(jnp.int32, (sp, sw), 1) // gc
    cmask = rg2 == lg2
    cr = jnp.where(cmask, cre[...], 0.0)
    ci = jnp.where(cmask, cim[...], 0.0)

    pr, pi = [jnp.ones_like(ar)], [jnp.zeros_like(ai)]
    for _ in range(L):
        nr, ni = _cmul(pr[-1], pi[-1], ar, ai)
        pr.append(nr)
        pi.append(ni)

    for j in range(L):
        xr, xi = _cmul(bbr, bbi, pr[L - 1 - j], pi[L - 1 - j])
        wst_ref[j * sw:(j + 1) * sw, 0:sp] = xr.astype(BF16)
        wst_ref[j * sw:(j + 1) * sw, sp:2 * sp] = xi.astype(BF16)

    kern = []
    for k in range(L):
        xr, xi = _cmul(bbr, bbi, pr[k], pi[k])
        kk = (jnp.dot(xr, cr, preferred_element_type=F32, precision=lax.Precision.HIGHEST)
              - jnp.dot(xi, ci, preferred_element_type=F32, precision=lax.Precision.HIGHEST))
        kern.append(kk.astype(BF16))
    zero = jnp.zeros((sw, sw), BF16)
    for j in range(L):
        for i in range(L):
            m_ref[j * sw:(j + 1) * sw, i * sw:(i + 1) * sw] = kern[i - j] if i >= j else zero

    acr, aci = discretise(lr_c[...], li_c[...], ldt_c[...])
    qr, qi = acr, aci
    for i in range(L):
        zr, zi = _cmul(cr, ci, qr, qi)
        wout_ref[0:sp, i * sw:(i + 1) * sw] = zr.astype(BF16)
        wout_ref[sp:2 * sp, i * sw:(i + 1) * sw] = (-zi).astype(BF16)
        qr, qi = _cmul(qr, qi, acr, aci)

    a8r, a8i = pr[L], pi[L]
    tr, ti = a8r, a8i
    for i in range(SUBLANES):
        pw_ref[i:i + 1, 0:sp] = tr
        pw_ref[i:i + 1, sp:2 * sp] = ti
        tr, ti = _cmul(tr, ti, a8r, a8i)


def _s5_tables(lam_re, lam_im, log_dt, b_re, b_im, c_re, c_im):
    g, p = lam_re.shape
    gc = b_re.shape[2]
    ns = g // S5_SLAB_GROUPS
    sw, sp = S5_SLAB_GROUPS * gc, S5_SLAB_GROUPS * p
    lw = S5_STEPS * sw

    def rows(a):
        return a.reshape(ns, 1, sp)

    def cols(a):
        return a.reshape(ns, sp, 1)

    ldt = jnp.broadcast_to(log_dt[:, None], (g, p))

    def b_layout(b):
        bt = jnp.transpose(b, (0, 2, 1)).reshape(ns, sw, p)
        return jnp.tile(bt, (1, 1, S5_SLAB_GROUPS))

    def c_layout(c):
        ct = jnp.transpose(c.reshape(ns, S5_SLAB_GROUPS, gc, p), (0, 3, 1, 2)).reshape(ns, p, sw)
        return jnp.tile(ct, (1, S5_SLAB_GROUPS, 1))

    row_spec = pl.BlockSpec((None, 1, sp), lambda s: (s, 0, 0))
    col_spec = pl.BlockSpec((None, sp, 1), lambda s: (s, 0, 0))
    b_spec = pl.BlockSpec((None, sw, sp), lambda s: (s, 0, 0))
    c_spec = pl.BlockSpec((None, sp, sw), lambda s: (s, 0, 0))
    kern = functools.partial(_s5_tables_kernel, gc=gc, p=p)
    return pl.pallas_call(
        kern,
        grid=(ns,),
        in_specs=[row_spec, row_spec, row_spec, col_spec, col_spec, col_spec,
                  b_spec, b_spec, c_spec, c_spec],
        out_specs=[
            pl.BlockSpec((None, lw, lw), lambda s: (s, 0, 0)),
            pl.BlockSpec((None, lw, 2 * sp), lambda s: (s, 0, 0)),
            pl.BlockSpec((None, 2 * sp, lw), lambda s: (s, 0, 0)),
            pl.BlockSpec((None, SUBLANES, 2 * sp), lambda s: (s, 0, 0)),
        ],
        out_shape=[
            jax.ShapeDtypeStruct((ns, lw, lw), BF16),
            jax.ShapeDtypeStruct((ns, lw, 2 * sp), BF16),
            jax.ShapeDtypeStruct((ns, 2 * sp, lw), BF16),
            jax.ShapeDtypeStruct((ns, SUBLANES, 2 * sp), F32),
        ],
        compiler_params=_cparams(("parallel",)),
        name="s5_tables",
    )(rows(lam_re), rows(lam_im), rows(ldt), cols(lam_re), cols(lam_im), cols(ldt),
      b_layout(b_re), b_layout(b_im), c_layout(c_re), c_layout(c_im))


def _s5_kernel(u_ref, m_ref, wst_ref, wout_ref, pw_ref, d_ref, wglu_ref, bglu_ref,
               o_ref, carry_ref, sloc_ref, sprev_ref, y_ref, out_ref, *, n_slabs, width):
    L = S5_STEPS
    sw = LANES
    rb = u_ref.shape[0]
    sp2 = sloc_ref.shape[1]
    sp = sp2 // 2

    @pl.when(pl.program_id(1) == 0)
    def _():
        carry_ref[...] = jnp.zeros_like(carry_ref)

    row_id = lax.broadcasted_iota(jnp.int32, (SUBLANES, sp2), 0)

    for s in range(n_slabs):
        u_slab = jnp.concatenate(
            [u_ref[:, j * width + s * sw: j * width + (s + 1) * sw] for j in range(L)], axis=1)
        sloc_ref[...] = _dot(u_slab, wst_ref[s])

        pw = pw_ref[s]
        pwr, pwi = pw[:, :sp], pw[:, sp:]

        def group(gi, carry, pwr=pwr, pwi=pwi):
            base = pl.multiple_of(gi * SUBLANES, SUBLANES)
            x = sloc_ref[pl.ds(base, SUBLANES), :]
            for sh in (1, 2, 4):
                sft = jnp.where(row_id >= sh, pltpu.roll(x, sh, axis=0), 0.0)
                ar = pwr[sh - 1:sh, :]
                ai = pwi[sh - 1:sh, :]
                tr, ti = _cmul(sft[:, :sp], sft[:, sp:], ar, ai)
                x = x + jnp.concatenate([tr, ti], axis=1)
            cr, ci = _cmul(carry[:, :sp], carry[:, sp:], pwr, pwi)
            incl = x + jnp.concatenate([cr, ci], axis=1)
            excl = jnp.where(row_id == 0, carry, pltpu.roll(incl, 1, axis=0))
            sprev_ref[pl.ds(base, SUBLANES), :] = excl
            return incl[SUBLANES - 1:SUBLANES, :]

        carry = lax.fori_loop(0, rb // SUBLANES, group, carry_ref[s])
        carry_ref[s] = carry

        y = _dot(u_slab, m_ref[s]) + _dot(sprev_ref[...].astype(BF16), wout_ref[s])
        for i in range(L):
            y_ref[:, i * width + s * sw: i * width + (s + 1) * sw] = y[:, i * sw:(i + 1) * sw]

    for i in range(L):
        cols = slice(i * width, (i + 1) * width)
        yi = y_ref[:, cols] + d_ref[...] * u_ref[:, cols].astype(F32)
        hh = _gelu_tanh(yi)
        gate = _sigmoid(_dot(hh.astype(BF16), wglu_ref[...]) + bglu_ref[...])
        val = hh * gate
        for c in range(width // LANES):
            out_ref[c, pl.ds(i, rb, stride=L), :] = val[:, c * LANES:(c + 1) * LANES]
    for c in range(width // LANES):
        o_ref[:, c * LANES:(c + 1) * LANES] = out_ref[c].astype(BF16)


def _s5(u3, m, wst, wout, pw, d_skip, wglu, bglu, *, rb):
    b, rows, lw = u3.shape
    ns = m.shape[0]
    width = lw // S5_STEPS
    sp2 = wst.shape[2]
    kern = functools.partial(_s5_kernel, n_slabs=ns, width=width)
    return pl.pallas_call(
        kern,
        grid=(b, rows // rb),
        in_specs=[
            pl.BlockSpec((None, rb, lw), lambda i, j: (i, j, 0)),
            _resident(m.shape), _resident(wst.shape), _resident(wout.shape), _resident(pw.shape),
            _resident(d_skip.shape), _resident(wglu.shape), _resident(bglu.shape),
        ],
        out_specs=pl.BlockSpec((None, rb * S5_STEPS, width), lambda i, j: (i, j, 0)),
        out_shape=jax.ShapeDtypeStruct((b, rows * S5_STEPS, width), BF16),
        scratch_shapes=[
            pltpu.VMEM((ns, 1, sp2), F32),
            pltpu.VMEM((rb, sp2), F32),
            pltpu.VMEM((rb, sp2), F32),
            pltpu.VMEM((rb, lw), F32),
            pltpu.VMEM((width // LANES, rb * S5_STEPS, LANES), F32),
        ],
        compiler_params=_cparams(("parallel", "arbitrary")),
        name="s5",
    )(u3, m, wst, wout, pw, d_skip, wglu, bglu)


def _mix_kernel(x_ref, ga_ref, gb_ref, oa_ref, ob_ref, wa_ref, wb_ref, wo_ref, o_ref):
    a = _dot(oa_ref[...], wa_ref[...])
    b = _dot(ob_ref[...], wb_ref[...])
    mix = _sigmoid(ga_ref[...].astype(F32)) * a + _sigmoid(gb_ref[...].astype(F32)) * b
    o_ref[...] = x_ref[...] + _dot(mix.astype(BF16), wo_ref[...])


def _mix(x2, main, o_a, o_b, w_a, w_b, w_o, *, tm):
    n, d = x2.shape
    return pl.pallas_call(
        _mix_kernel,
        grid=(n // tm,),
        in_specs=[
            pl.BlockSpec((tm, d), lambda i: (i, 0)),
            pl.BlockSpec((tm, d), lambda i: (i, 0)),
            pl.BlockSpec((tm, d), lambda i: (i, 1)),
            pl.BlockSpec((tm, o_a.shape[1]), lambda i: (i, 0)),
            pl.BlockSpec((tm, o_b.shape[1]), lambda i: (i, 0)),
            _resident(w_a.shape), _resident(w_b.shape), _resident(w_o.shape),
        ],
        out_specs=pl.BlockSpec((tm, d), lambda i: (i, 0)),
        out_shape=jax.ShapeDtypeStruct((n, d), F32),
        compiler_params=_cparams(("parallel",)),
        name="mix",
    )(x2, main, main, o_a, o_b, w_a, w_b, w_o)


def _ffn_kernel(x_ref, g2_ref, wg_ref, wu_ref, wd_ref, gf_ref, o_ref, h_ref):
    j = pl.program_id(1)

    @pl.when(j == 0)
    def _():
        x = x_ref[...]
        ms = jnp.mean(x * x, axis=-1, keepdims=True)
        h_ref[...] = (x * lax.rsqrt(ms + NORM_EPS) * g2_ref[...]).astype(BF16)
        o_ref[...] = x

    h = h_ref[...]
    gate = _dot(h, wg_ref[...])
    up = _dot(h, wu_ref[...])
    act = (gate * _sigmoid(gate) * up).astype(BF16)
    o_ref[...] += _dot(act, wd_ref[...])

    @pl.when(j == pl.num_programs(1) - 1)
    def _():
        y = o_ref[...]
        ms = jnp.mean(y * y, axis=-1, keepdims=True)
        o_ref[...] = y * lax.rsqrt(ms + NORM_EPS) * gf_ref[...]


def _ffn(x1, g2, w_in, w_out, gf, *, tm, tf):
    n, d = x1.shape
    f = w_out.shape[0]
    nf = f // tf
    return pl.pallas_call(
        _ffn_kernel,
        grid=(n // tm, nf),
        in_specs=[
            pl.BlockSpec((tm, d), lambda i, j: (i, 0)),
            pl.BlockSpec((1, d), lambda i, j: (0, 0)),
            pl.BlockSpec((d, tf), lambda i, j: (0, j)),
            pl.BlockSpec((d, tf), lambda i, j: (0, j + nf)),
            pl.BlockSpec((tf, d), lambda i, j: (j, 0)),
            pl.BlockSpec((1, d), lambda i, j: (0, 0)),
        ],
        out_specs=pl.BlockSpec((tm, d), lambda i, j: (i, 0)),
        out_shape=jax.ShapeDtypeStruct((n, d), F32),
        scratch_shapes=[pltpu.VMEM((tm, d), BF16)],
        compiler_params=_cparams(("parallel", "arbitrary")),
        name="ffn",
    )(x1, g2, w_in, w_in, w_out, gf)


def _tile(n, pref):
    t = min(n, pref)
    assert n % t == 0, (n, t)
    return t


def _layer(x2, bsz, seq, norm1_g, w_in, w_a2, b_a2, gla_norm_g, lam_re, lam_im, log_dt,
           s5_b_re, s5_b_im, s5_c_re, s5_c_im, s5_d, w_glu, b_glu,
           w_branch_a, w_branch_b, w_out, norm2_g, w_ffn_in, w_ffn_out, final_g):
    n, d = x2.shape
    hk = w_a2.shape[1]
    hv = gla_norm_g.shape[0]
    rank = w_a2.shape[0]
    sw = w_glu.shape[0]
    dk, dv = hk // GLA_HEADS, hv // GLA_HEADS

    o_q, o_k, o_v, o_r = 0, hk, 2 * hk, 2 * hk + hv
    o_al = 2 * hk + 2 * hv
    o_u = o_al + rank
    o_ga = o_u + sw
    o_gb = o_ga + d
    w_main = jnp.concatenate(
        [w_in[:, o_ga:o_ga + d], w_in[:, o_gb:o_gb + d], w_in[:, o_v:o_v + hv], w_in[:, o_r:o_r + hv],
         w_in[:, o_q:o_q + hk], w_in[:, o_k:o_k + hk]], axis=1).astype(BF16)
    w_u = w_in[:, o_u:o_u + sw].astype(BF16)
    w_low = jnp.pad(w_in[:, o_al:o_al + rank], ((0, 0), (0, LANES - rank))).astype(BF16)
    col_v, col_r, col_q, col_k = 2 * d, 2 * d + hv, 2 * d + 2 * hv, 2 * d + 2 * hv + hk

    main, u, alow = _in_proj(x2, norm1_g.reshape(1, d), w_main, w_u, w_low,
                             tm=_tile(n, 1024), tn=1024)

    tc = _tile(seq, 256)
    blk = jnp.arange(tc) // GLA_CHUNK
    tri = ((blk[:, None] == blk[None, :]) & (jnp.arange(tc)[:, None] >= jnp.arange(tc)[None, :]))
    wa2p = jnp.pad(w_a2, ((0, LANES - rank), (0, 0)))
    o_a = _gla(main.reshape(bsz, seq, -1), alow.reshape(bsz, seq, LANES), wa2p,
               b_a2.reshape(1, hk), gla_norm_g.reshape(1, hv), tri.astype(BF16),
               tc=tc, col_q=col_q, col_k=col_k, col_v=col_v, col_r=col_r, dk=dk, dv=dv)

    m, wst, wout, pw = _s5_tables(lam_re, lam_im, log_dt, s5_b_re, s5_b_im, s5_c_re, s5_c_im)
    rows = seq // S5_STEPS
    o_b = _s5(u.reshape(bsz, rows, S5_STEPS * sw), m, wst, wout, pw.reshape(pw.shape[0], SUBLANES, -1),
              s5_d.reshape(1, sw), w_glu.astype(BF16), b_glu.reshape(1, sw), rb=_tile(rows, 256))

    x1 = _mix(x2, main, o_a.reshape(n, hv), o_b.reshape(n, sw), w_branch_a.astype(BF16),
              w_branch_b.astype(BF16), w_out.astype(BF16), tm=_tile(n, 512))
    return _ffn(x1, norm2_g.reshape(1, d), w_ffn_in.astype(BF16), w_ffn_out.astype(BF16),
                final_g.reshape(1, d), tm=_tile(n, 1024), tf=512)


def kernel(x, norm1_g, w_in, w_a2, b_a2, gla_norm_g, lam_re, lam_im, log_dt, s5_b_re, s5_b_im,
           s5_c_re, s5_c_im, s5_d, w_glu, b_glu, w_branch_a, w_branch_b, w_out, norm2_g,
           w_ffn_in, w_ffn_out, final_norm_g):
    bsz, seq, d = x.shape
    assert norm1_g.shape[0] == 1, "single-layer block"
    out = _layer(x.reshape(bsz * seq, d), bsz, seq, norm1_g[0], w_in[0], w_a2[0], b_a2[0],
                 gla_norm_g[0], lam_re[0], lam_im[0], log_dt[0], s5_b_re[0], s5_b_im[0],
                 s5_c_re[0], s5_c_im[0], s5_d[0], w_glu[0], b_glu[0], w_branch_a[0],
                 w_branch_b[0], w_out[0], norm2_g[0], w_ffn_in[0], w_ffn_out[0], final_norm_g[0])
    return out.reshape(bsz, seq, d)
```

```python
import functools
import math

import jax
import jax.numpy as jnp
from jax import lax
from jax.experimental import pallas as pl
from jax.experimental.pallas import tpu as pltpu

F32 = jnp.float32
BF16 = jnp.bfloat16

NORM_EPS = 1e-6
GLA_HEADS = 4
GLA_CHUNK = 32
GLA_GATE_TAU = 16.0
LANES = 128
SUBLANES = 8
S5_STEPS = 8
S5_SLAB_GROUPS = 8
VMEM_LIMIT = 56 * 1024 * 1024


def _sigmoid(x):
    return 1.0 / (1.0 + jnp.exp(-x))


def _log_sigmoid(x):
    return jnp.minimum(x, 0.0) - jnp.log1p(jnp.exp(-jnp.abs(x)))


def _gelu_tanh(x):
    c = math.sqrt(2.0 / math.pi)
    return 0.5 * x * (1.0 + jnp.tanh(c * (x + 0.044715 * (x * x * x))))


def _dot(a, b):
    return jnp.dot(a, b, preferred_element_type=F32)


def _cparams(sem):
    return pltpu.CompilerParams(dimension_semantics=sem, vmem_limit_bytes=VMEM_LIMIT)


def _resident(shape):
    zeros = (0,) * len(shape)
    return pl.BlockSpec(shape, lambda *_: zeros, pipeline_mode=pl.Buffered(1))


def _in_proj_kernel(x_ref, g_ref, w_ref, wu_ref, wlow_ref, main_ref, u_ref, alow_ref, h_ref, uacc_ref,
                    *, n_main):
    j = pl.program_id(1)

    @pl.when(j == 0)
    def _():
        x = x_ref[...]
        ms = jnp.mean(x * x, axis=-1, keepdims=True)
        h_ref[...] = (x * lax.rsqrt(ms + NORM_EPS) * g_ref[...]).astype(BF16)

    @pl.when(j < n_main)
    def _():
        main_ref[...] = _dot(h_ref[...], w_ref[...]).astype(BF16)

    @pl.when(j == n_main)
    def _():
        h = h_ref[...]
        alow_ref[...] = _dot(h, wlow_ref[...])
        uacc = _dot(h, wu_ref[...])
        rows, width = u_ref.shape[0], wu_ref.shape[1]
        for c in range(width // LANES):
            uacc_ref[c] = uacc[:, c * LANES:(c + 1) * LANES]
        for s in range(S5_STEPS):
            for c in range(width // LANES):
                lo = s * width + c * LANES
                u_ref[:, lo:lo + LANES] = uacc_ref[c, pl.ds(s, rows, stride=S5_STEPS), :].astype(BF16)


def _in_proj(x2, g, w_main, w_u, w_low, *, tm, tn):
    n, d = x2.shape
    n_main = w_main.shape[1] // tn
    sw = w_u.shape[1]
    kern = functools.partial(_in_proj_kernel, n_main=n_main)
    return pl.pallas_call(
        kern,
        grid=(n // tm, n_main + 1),
        in_specs=[
            pl.BlockSpec((tm, d), lambda i, j: (i, 0)),
            pl.BlockSpec((1, d), lambda i, j: (0, 0)),
            pl.BlockSpec((d, tn), lambda i, j: (0, jnp.minimum(j, n_main - 1))),
            _resident(w_u.shape),
            _resident(w_low.shape),
        ],
        out_specs=[
            pl.BlockSpec((tm, tn), lambda i, j: (i, jnp.minimum(j, n_main - 1))),
            pl.BlockSpec((tm // S5_STEPS, S5_STEPS * sw), lambda i, j: (i, 0)),
            pl.BlockSpec((tm, LANES), lambda i, j: (i, 0)),
        ],
        out_shape=[
            jax.ShapeDtypeStruct((n, n_main * tn), BF16),
            jax.ShapeDtypeStruct((n // S5_STEPS, S5_STEPS * sw), BF16),
            jax.ShapeDtypeStruct((n, LANES), F32),
        ],
        scratch_shapes=[pltpu.VMEM((tm, d), BF16), pltpu.VMEM((sw // LANES, tm, LANES), F32)],
        compiler_params=_cparams(("parallel", "arbitrary")),
        name="in_proj",
    )(x2, g, w_main, w_u, w_low)


def _kr_base(m):
    return GLA_CHUNK * (m * (m + 1) // 2)


def _gla_kernel(q_ref, k_ref, v_ref, r_ref, alow_ref, wa2_ref, ba2_ref, ghn_ref, tri_ref,
                o_ref, s_ref, qd_ref, qb_ref, kb_ref, kr_ref, p_ref, *, n_sub, dk, dv):
    c = GLA_CHUNK

    @pl.when(pl.program_id(1) == 0)
    def _():
        s_ref[...] = jnp.zeros_like(s_ref)
        p_ref[...] = jnp.zeros_like(p_ref)

    z = _dot(alow_ref[...], wa2_ref[...]) + ba2_ref[...]
    log_a = _log_sigmoid(z) * (1.0 / GLA_GATE_TAU)
    la_hi = log_a.astype(BF16)
    la_lo = (log_a - la_hi.astype(F32)).astype(BF16)
    tri = tri_ref[...]
    bc_all = _dot(tri, la_hi) + _dot(tri, la_lo)

    bl = [bc_all[(i + 1) * c - 1:(i + 1) * c, :] for i in range(n_sub)]
    bs = [jnp.zeros_like(bl[0])]
    for i in range(n_sub):
        bs.append(bs[-1] + bl[i])
    btot = bs[n_sub]
    scale = dk ** -0.5

    for i in range(n_sub):
        rows = slice(i * c, (i + 1) * c)
        b = bc_all[rows, :]
        q = q_ref[rows, :].astype(F32) * scale
        k = k_ref[rows, :].astype(F32)
        qd = q * jnp.exp(b)
        ks = k * jnp.exp(bl[i] - b)
        qd_ref[rows, :] = qd.astype(BF16)
        qb_ref[rows, :] = (qd * jnp.exp(bs[i])).astype(BF16)
        kb_ref[rows, :] = (ks * jnp.exp(btot - bs[i + 1])).astype(BF16)
        own = _kr_base(i) + i * c
        kr_ref[own:own + c, :] = (k * jnp.exp(-b)).astype(BF16)
        for m in range(i + 1, n_sub):
            lo = _kr_base(m) + i * c
            kr_ref[lo:lo + c, :] = (ks * jnp.exp(bs[m] - bs[i + 1])).astype(BF16)

    row = lax.broadcasted_iota(jnp.int32, (c, c), 0)
    col = lax.broadcasted_iota(jnp.int32, (c, c), 1)
    causal = row >= col

    for h in range(GLA_HEADS):
        ks_ = slice(h * dk, (h + 1) * dk)
        vs_ = slice(h * dv, (h + 1) * dv)
        for m in range(n_sub):
            rows = slice(m * c, (m + 1) * c)
            keys = slice(_kr_base(m), _kr_base(m) + (m + 1) * c)
            s = lax.dot_general(qd_ref[rows, ks_], kr_ref[keys, ks_], (((1,), (1,)), ((), ())),
                                preferred_element_type=F32)
            if m > 0:
                p_ref[h, rows, 0:m * c] = s[:, 0:m * c].astype(BF16)
            p_ref[h, rows, m * c:(m + 1) * c] = jnp.where(causal, s[:, m * c:], 0.0).astype(BF16)

        v = v_ref[:, vs_]
        state = s_ref[h]
        o = _dot(p_ref[h], v) + _dot(qb_ref[:, ks_], state.astype(BF16))

        upd = lax.dot_general(kb_ref[:, ks_], v, (((0,), (0,)), ((), ())),
                              preferred_element_type=F32)
        decay_col = jnp.transpose(jnp.broadcast_to(jnp.exp(btot[:, ks_]), (dk, dk)))
        if dv > dk:
            decay_col = jnp.concatenate([decay_col] * (dv // dk), axis=1)
        s_ref[h] = decay_col * state + upd

        o = o * lax.rsqrt(jnp.mean(o * o, axis=-1, keepdims=True) + NORM_EPS)
        o = o * ghn_ref[:, vs_]
        r = r_ref[:, vs_].astype(F32)
        o_ref[:, vs_] = (r * _sigmoid(r) * o).astype(BF16)


def _gla(main3, alow3, wa2p, ba2, ghn, tri, *, tc, col_q, col_k, col_v, col_r, dk, dv):
    b, t, _ = main3.shape
    hk, hv = GLA_HEADS * dk, GLA_HEADS * dv
    kern = functools.partial(_gla_kernel, n_sub=tc // GLA_CHUNK, dk=dk, dv=dv)
    return pl.pallas_call(
        kern,
        grid=(b, t // tc),
        in_specs=[
            pl.BlockSpec((None, tc, hk), lambda i, j: (i, j, col_q // hk)),
            pl.BlockSpec((None, tc, hk), lambda i, j: (i, j, col_k // hk)),
            pl.BlockSpec((None, tc, hv), lambda i, j: (i, j, col_v // hv)),
            pl.BlockSpec((None, tc, hv), lambda i, j: (i, j, col_r // hv)),
            pl.BlockSpec((None, tc, LANES), lambda i, j: (i, j, 0)),
            pl.BlockSpec((LANES, hk), lambda i, j: (0, 0)),
            pl.BlockSpec((1, hk), lambda i, j: (0, 0)),
            pl.BlockSpec((1, hv), lambda i, j: (0, 0)),
            pl.BlockSpec((tc, tc), lambda i, j: (0, 0)),
        ],
        out_specs=pl.BlockSpec((None, tc, hv), lambda i, j: (i, j, 0)),
        out_shape=jax.ShapeDtypeStruct((b, t, hv), BF16),
        scratch_shapes=[
            pltpu.VMEM((GLA_HEADS, dk, dv), F32),
            pltpu.VMEM((tc, hk), BF16),
            pltpu.VMEM((tc, hk), BF16),
            pltpu.VMEM((tc, hk), BF16),
            pltpu.VMEM((_kr_base(tc // GLA_CHUNK), hk), BF16),
            pltpu.VMEM((GLA_HEADS, tc, tc), BF16),
        ],
        compiler_params=_cparams(("parallel", "arbitrary")),
        name="gla",
    )(main3, main3, main3, main3, alow3, wa2p, ba2, ghn, tri)


def _cmul(ar, ai, br, bi):
    return ar * br - ai * bi, ar * bi + ai * br


def _s5_tables_kernel(lr_r, li_r, ldt_r, lr_c, li_c, ldt_c, bre, bim, cre, cim,
                      m_ref, wst_ref, wout_ref, pw_ref, *, gc, p):
    L = S5_STEPS
    sw = S5_SLAB_GROUPS * gc
    sp = S5_SLAB_GROUPS * p

    def discretise(lr, li, ldt):
        dt = jnp.exp(ldt)
        mag = jnp.exp(lr * dt)
        return mag * jnp.cos(li * dt), mag * jnp.sin(li * dt)

    lr, li = lr_r[...], li_r[...]
    ar, ai = discretise(lr, li, ldt_r[...])
    den = lr * lr + li * li
    am1 = ar - 1.0
    f_re = (am1 * lr + ai * li) / den
    f_im = (ai * lr - am1 * li) / den

    rg = lax.broadcasted_iota(jnp.int32, (sw, sp), 0) // gc
    lg = lax.broadcasted_iota(jnp.int32, (sw, sp), 1) // p
    bmask = rg == lg
    br, bi = bre[...], bim[...]
    bbr = jnp.where(bmask, f_re * br - f_im * bi, 0.0)
    bbi = jnp.where(bmask, f_re * bi + f_im * br, 0.0)

    rg2 = lax.broadcasted_iota(jnp.int32, (sp, sw), 0) // p
    lg2 = lax.broadcasted_iota(jnp.int32, (sp, sw), 1) // gc
    cmask = rg2 == lg2
    cr = jnp.where(cmask, cre[...], 0.0)
    ci = jnp.where(cmask, cim[...], 0.0)

    pr, pi = [jnp.ones_like(ar)], [jnp.zeros_like(ai)]
    for _ in range(L):
        nr, ni = _cmul(pr[-1], pi[-1], ar, ai)
        pr.append(nr)
        pi.append(ni)

    for j in range(L):
        xr, xi = _cmul(bbr, bbi, pr[L - 1 - j], pi[L - 1 - j])
        wst_ref[j * sw:(j + 1) * sw, 0:sp] = xr.astype(BF16)
        wst_ref[j * sw:(j + 1) * sw, sp:2 * sp] = xi.astype(BF16)

    kern = []
    for k in range(L):
        xr, xi = _cmul(bbr, bbi, pr[k], pi[k])
        kk = (jnp.dot(xr, cr, preferred_element_type=F32, precision=lax.Precision.HIGHEST)
              - jnp.dot(xi, ci, preferred_element_type=F32, precision=lax.Precision.HIGHEST))
        kern.append(kk.astype(BF16))
    zero = jnp.zeros((sw, sw), BF16)
    for j in range(L):
        for i in range(L):
            m_ref[j * sw:(j + 1) * sw, i * sw:(i + 1) * sw] = kern[i - j] if i >= j else zero

    acr, aci = discretise(lr_c[...], li_c[...], ldt_c[...])
    qr, qi = acr, aci
    for i in range(L):
        zr, zi = _cmul(cr, ci, qr, qi)
        wout_ref[0:sp, i * sw:(i + 1) * sw] = zr.astype(BF16)
        wout_ref[sp:2 * sp, i * sw:(i + 1) * sw] = (-zi).astype(BF16)
        qr, qi = _cmul(qr, qi, acr, aci)

    a8r, a8i = pr[L], pi[L]
    tr, ti = a8r, a8i
    for i in range(SUBLANES):
        pw_ref[i:i + 1, 0:sp] = tr
        pw_ref[i:i + 1, sp:2 * sp] = ti
        tr, ti = _cmul(tr, ti, a8r, a8i)


def _s5_tables(lam_re, lam_im, log_dt, b_re, b_im, c_re, c_im):
    g, p = lam_re.shape
    gc = b_re.shape[2]
    ns = g // S5_SLAB_GROUPS
    sw, sp = S5_SLAB_GROUPS * gc, S5_SLAB_GROUPS * p
    lw = S5_STEPS * sw

    def rows(a):
        return a.reshape(ns, 1, sp)

    def cols(a):
        return a.reshape(ns, sp, 1)

    ldt = jnp.broadcast_to(log_dt[:, None], (g, p))

    def b_layout(b):
        bt = jnp.transpose(b, (0, 2, 1)).reshape(ns, sw, p)
        return jnp.tile(bt, (1, 1, S5_SLAB_GROUPS))

    def c_layout(c):
        ct = jnp.transpose(c.reshape(ns, S5_SLAB_GROUPS, gc, p), (0, 3, 1, 2)).reshape(ns, p, sw)
        return jnp.tile(ct, (1, S5_SLAB_GROUPS, 1))

    row_spec = pl.BlockSpec((None, 1, sp), lambda s: (s, 0, 0))
    col_spec = pl.BlockSpec((None, sp, 1), lambda s: (s, 0, 0))
    b_spec = pl.BlockSpec((None, sw, sp), lambda s: (s, 0, 0))
    c_spec = pl.BlockSpec((None, sp, sw), lambda s: (s, 0, 0))
    kern = functools.partial(_s5_tables_kernel, gc=gc, p=p)
    return pl.pallas_call(
        kern,
        grid=(ns,),
        in_specs=[row_spec, row_spec, row_spec, col_spec, col_spec, col_spec,
                  b_spec, b_spec, c_spec, c_spec],
        out_specs=[
            pl.BlockSpec((None, lw, lw), lambda s: (s, 0, 0)),
            pl.BlockSpec((None, lw, 2 * sp), lambda s: (s, 0, 0)),
            pl.BlockSpec((None, 2 * sp, lw), lambda s: (s, 0, 0)),
            pl.BlockSpec((None, SUBLANES, 2 * sp), lambda s: (s, 0, 0)),
        ],
        out_shape=[
            jax.ShapeDtypeStruct((ns, lw, lw), BF16),
            jax.ShapeDtypeStruct((ns, lw, 2 * sp), BF16),
            jax.ShapeDtypeStruct((ns, 2 * sp, lw), BF16),
            jax.ShapeDtypeStruct((ns, SUBLANES, 2 * sp), F32),
        ],
        compiler_params=_cparams(("parallel",)),
        name="s5_tables",
    )(rows(lam_re), rows(lam_im), rows(ldt), cols(lam_re), cols(lam_im), cols(ldt),
      b_layout(b_re), b_layout(b_im), c_layout(c_re), c_layout(c_im))


def _s5_kernel(u_ref, m_ref, wst_ref, wout_ref, pw_ref, d_ref, wglu_ref, bglu_ref,
               o_ref, carry_ref, sloc_ref, sprev_ref, y_ref, out_ref, *, n_slabs, width):
    L = S5_STEPS
    sw = LANES
    rb = u_ref.shape[0]
    sp2 = sloc_ref.shape[2]
    sp = sp2 // 2

    @pl.when(pl.program_id(1) == 0)
    def _():
        carry_ref[...] = jnp.zeros_like(carry_ref)

    row_id = lax.broadcasted_iota(jnp.int32, (SUBLANES, sp2), 0)

    def slab_input(s):
        return jnp.concatenate(
            [u_ref[:, j * width + s * sw: j * width + (s + 1) * sw] for j in range(L)], axis=1)

    for s in range(n_slabs):
        sloc_ref[s] = _dot(slab_input(s), wst_ref[s])

    pws = [pw_ref[s] for s in range(n_slabs)]
    carries = [carry_ref[s] for s in range(n_slabs)]
    for gi in range(rb // SUBLANES):
        grp = slice(gi * SUBLANES, (gi + 1) * SUBLANES)
        for s in range(n_slabs):
            pwr, pwi = pws[s][:, :sp], pws[s][:, sp:]
            carry = carries[s]
            x = sloc_ref[s, grp, :]
            for sh in (1, 2, 4):
                sft = jnp.where(row_id >= sh, pltpu.roll(x, sh, axis=0), 0.0)
                tr, ti = _cmul(sft[:, :sp], sft[:, sp:], pwr[sh - 1:sh, :], pwi[sh - 1:sh, :])
                x = x + jnp.concatenate([tr, ti], axis=1)
            cr, ci = _cmul(carry[:, :sp], carry[:, sp:], pwr, pwi)
            incl = x + jnp.concatenate([cr, ci], axis=1)
            sprev_ref[s, grp, :] = jnp.where(row_id == 0, carry, pltpu.roll(incl, 1, axis=0))
            carries[s] = incl[SUBLANES - 1:SUBLANES, :]
    for s in range(n_slabs):
        carry_ref[s] = carries[s]

    for s in range(n_slabs):
        y = _dot(slab_input(s), m_ref[s]) + _dot(sprev_ref[s].astype(BF16), wout_ref[s])
        for i in range(L):
            y_ref[:, i * width + s * sw: i * width + (s + 1) * sw] = y[:, i * sw:(i + 1) * sw]

    for i in range(L):
        cols = slice(i * width, (i + 1) * width)
        yi = y_ref[:, cols] + d_ref[...] * u_ref[:, cols].astype(F32)
        hh = _gelu_tanh(yi)
        gate = _sigmoid(_dot(hh.astype(BF16), wglu_ref[...]) + bglu_ref[...])
        val = hh * gate
        for c in range(width // LANES):
            out_ref[c, pl.ds(i, rb, stride=L), :] = val[:, c * LANES:(c + 1) * LANES]
    for c in range(width // LANES):
        o_ref[:, c * LANES:(c + 1) * LANES] = out_ref[c].astype(BF16)


def _s5(u3, m, wst, wout, pw, d_skip, wglu, bglu, *, rb):
    b, rows, lw = u3.shape
    ns = m.shape[0]
    width = lw // S5_STEPS
    sp2 = wst.shape[2]
    kern = functools.partial(_s5_kernel, n_slabs=ns, width=width)
    return pl.pallas_call(
        kern,
        grid=(b, rows // rb),
        in_specs=[
            pl.BlockSpec((None, rb, lw), lambda i, j: (i, j, 0)),
            _resident(m.shape), _resident(wst.shape), _resident(wout.shape), _resident(pw.shape),
            _resident(d_skip.shape), _resident(wglu.shape), _resident(bglu.shape),
        ],
        out_specs=pl.BlockSpec((None, rb * S5_STEPS, width), lambda i, j: (i, j, 0)),
        out_shape=jax.ShapeDtypeStruct((b, rows * S5_STEPS, width), BF16),
        scratch_shapes=[
            pltpu.VMEM((ns, 1, sp2), F32),
            pltpu.VMEM((ns, rb, sp2), F32),
            pltpu.VMEM((ns, rb, sp2), F32),
            pltpu.VMEM((rb, lw), F32),
            pltpu.VMEM((width // LANES, rb * S5_STEPS, LANES), F32),
        ],
        compiler_params=_cparams(("parallel", "arbitrary")),
        name="s5",
    )(u3, m, wst, wout, pw, d_skip, wglu, bglu)


def _mix_kernel(x_ref, ga_ref, gb_ref, oa_ref, ob_ref, wa_ref, wb_ref, wo_ref, o_ref):
    a = _dot(oa_ref[...], wa_ref[...])
    b = _dot(ob_ref[...], wb_ref[...])
    mix = _sigmoid(ga_ref[...].astype(F32)) * a + _sigmoid(gb_ref[...].astype(F32)) * b
    o_ref[...] = x_ref[...] + _dot(mix.astype(BF16), wo_ref[...])


def _mix(x2, main, o_a, o_b, w_a, w_b, w_o, *, tm):
    n, d = x2.shape
    return pl.pallas_call(
        _mix_kernel,
        grid=(n // tm,),
        in_specs=[
            pl.BlockSpec((tm, d), lambda i: (i, 0)),
            pl.BlockSpec((tm, d), lambda i: (i, 0)),
            pl.BlockSpec((tm, d), lambda i: (i, 1)),
            pl.BlockSpec((tm, o_a.shape[1]), lambda i: (i, 0)),
            pl.BlockSpec((tm, o_b.shape[1]), lambda i: (i, 0)),
            _resident(w_a.shape), _resident(w_b.shape), _resident(w_o.shape),
        ],
        out_specs=pl.BlockSpec((tm, d), lambda i: (i, 0)),
        out_shape=jax.ShapeDtypeStruct((n, d), F32),
        compiler_params=_cparams(("parallel",)),
        name="mix",
    )(x2, main, main, o_a, o_b, w_a, w_b, w_o)


def _ffn_kernel(x_ref, g2_ref, wg_ref, wu_ref, wd_ref, gf_ref, o_ref, h_ref):
    j = pl.program_id(1)

    @pl.when(j == 0)
    def _():
        x = x_ref[...]
        ms = jnp.mean(x * x, axis=-1, keepdims=True)
        h_ref[...] = (x * lax.rsqrt(ms + NORM_EPS) * g2_ref[...]).astype(BF16)
        o_ref[...] = x

    h = h_ref[...]
    gate = _dot(h, wg_ref[...])
    up = _dot(h, wu_ref[...])
    act = (gate * _sigmoid(gate) * up).astype(BF16)
    o_ref[...] += _dot(act, wd_ref[...])

    @pl.when(j == pl.num_programs(1) - 1)
    def _():
        y = o_ref[...]
        ms = jnp.mean(y * y, axis=-1, keepdims=True)
        o_ref[...] = y * lax.rsqrt(ms + NORM_EPS) * gf_ref[...]


def _ffn(x1, g2, w_in, w_out, gf, *, tm, tf):
    n, d = x1.shape
    f = w_out.shape[0]
    nf = f // tf
    return pl.pallas_call(
        _ffn_kernel,
        grid=(n // tm, nf),
        in_specs=[
            pl.BlockSpec((tm, d), lambda i, j: (i, 0)),
            pl.BlockSpec((1, d), lambda i, j: (0, 0)),
            pl.BlockSpec((d, tf), lambda i, j: (0, j)),
            pl.BlockSpec((d, tf), lambda i, j: (0, j + nf)),
            pl.BlockSpec((tf, d), lambda i, j: (j, 0)),
            pl.BlockSpec((1, d), lambda i, j: (0, 0)),
        ],
        out_specs=pl.BlockSpec((tm, d), lambda i, j: (i, 0)),
        out_shape=jax.ShapeDtypeStruct((n, d), F32),
        scratch_shapes=[pltpu.VMEM((tm, d), BF16)],
        compiler_params=_cparams(("parallel", "arbitrary")),
        name="ffn",
    )(x1, g2, w_in, w_in, w_out, gf)


def _tile(n, pref):
    t = min(n, pref)
    assert n % t == 0, (n, t)
    return t


def _layer(x2, bsz, seq, norm1_g, w_in, w_a2, b_a2, gla_norm_g, lam_re, lam_im, log_dt,
           s5_b_re, s5_b_im, s5_c_re, s5_c_im, s5_d, w_glu, b_glu,
           w_branch_a, w_branch_b, w_out, norm2_g, w_ffn_in, w_ffn_out, final_g):
    n, d = x2.shape
    hk = w_a2.shape[1]
    hv = gla_norm_g.shape[0]
    rank = w_a2.shape[0]
    sw = w_glu.shape[0]
    dk, dv = hk // GLA_HEADS, hv // GLA_HEADS

    o_q, o_k, o_v, o_r = 0, hk, 2 * hk, 2 * hk + hv
    o_al = 2 * hk + 2 * hv
    o_u = o_al + rank
    o_ga = o_u + sw
    o_gb = o_ga + d
    w_main = jnp.concatenate(
        [w_in[:, o_ga:o_ga + d], w_in[:, o_gb:o_gb + d], w_in[:, o_v:o_v + hv], w_in[:, o_r:o_r + hv],
         w_in[:, o_q:o_q + hk], w_in[:, o_k:o_k + hk]], axis=1).astype(BF16)
    w_u = w_in[:, o_u:o_u + sw].astype(BF16)
    w_low = jnp.pad(w_in[:, o_al:o_al + rank], ((0, 0), (0, LANES - rank))).astype(BF16)
    col_v, col_r, col_q, col_k = 2 * d, 2 * d + hv, 2 * d + 2 * hv, 2 * d + 2 * hv + hk

    main, u, alow = _in_proj(x2, norm1_g.reshape(1, d), w_main, w_u, w_low,
                             tm=_tile(n, 1024), tn=1024)

    tc = _tile(seq, 256)
    blk = jnp.arange(tc) // GLA_CHUNK
    tri = ((blk[:, None] == blk[None, :]) & (jnp.arange(tc)[:, None] >= jnp.arange(tc)[None, :]))
    wa2p = jnp.pad(w_a2, ((0, LANES - rank), (0, 0)))
    o_a = _gla(main.reshape(bsz, seq, -1), alow.reshape(bsz, seq, LANES), wa2p,
               b_a2.reshape(1, hk), gla_norm_g.reshape(1, hv), tri.astype(BF16),
               tc=tc, col_q=col_q, col_k=col_k, col_v=col_v, col_r=col_r, dk=dk, dv=dv)

    m, wst, wout, pw = _s5_tables(lam_re, lam_im, log_dt, s5_b_re, s5_b_im, s5_c_re, s5_c_im)
    rows = seq // S5_STEPS
    o_b = _s5(u.reshape(bsz, rows, S5_STEPS * sw), m, wst, wout, pw.reshape(pw.shape[0], SUBLANES, -1),
              s5_d.reshape(1, sw), w_glu.astype(BF16), b_glu.reshape(1, sw), rb=_tile(rows, 256))

    x1 = _mix(x2, main, o_a.reshape(n, hv), o_b.reshape(n, sw), w_branch_a.astype(BF16),
              w_branch_b.astype(BF16), w_out.astype(BF16), tm=_tile(n, 512))
    return _ffn(x1, norm2_g.reshape(1, d), w_ffn_in.astype(BF16), w_ffn_out.astype(BF16),
                final_g.reshape(1, d), tm=_tile(n, 1024), tf=512)


def kernel(x, norm1_g, w_in, w_a2, b_a2, gla_norm_g, lam_re, lam_im, log_dt, s5_b_re, s5_b_im,
           s5_c_re, s5_c_im, s5_d, w_glu, b_glu, w_branch_a, w_branch_b, w_out, norm2_g,
           w_ffn_in, w_ffn_out, final_norm_g):
    bsz, seq, d = x.shape
    assert norm1_g.shape[0] == 1, "single-layer block"
    out = _layer(x.reshape(bsz * seq, d), bsz, seq, norm1_g[0], w_in[0], w_a2[0], b_a2[0],
                 gla_norm_g[0], lam_re[0], lam_im[0], log_dt[0], s5_b_re[0], s5_b_im[0],
                 s5_c_re[0], s5_c_im[0], s5_d[0], w_glu[0], b_glu[0], w_branch_a[0],
                 w_branch_b[0], w_out[0], norm2_g[0], w_ffn_in[0], w_ffn_out[0], final_norm_g[0])
    return out.reshape(bsz, seq, d)
```

```python
import functools
import math

import jax
import jax.numpy as jnp
from jax import lax
from jax.experimental import pallas as pl
from jax.experimental.pallas import tpu as pltpu

F32 = jnp.float32
BF16 = jnp.bfloat16

NORM_EPS = 1e-6
GLA_HEADS = 4
GLA_CHUNK = 32
GLA_GATE_TAU = 16.0
LANES = 128
SUBLANES = 8
MXU_TILE = 256
S5_STEPS = 8
S5_SLAB_GROUPS = 8
VMEM_LIMIT = 56 * 1024 * 1024


def _sigmoid(x):
    return 1.0 / (1.0 + jnp.exp(-x))


def _log_sigmoid(x):
    return jnp.minimum(x, 0.0) - jnp.log(1.0 + jnp.exp(-jnp.abs(x)))


def _gelu_tanh(x):
    c = math.sqrt(2.0 / math.pi)
    return 0.5 * x * (1.0 + jnp.tanh(c * (x + 0.044715 * (x * x * x))))


def _dot(a, b):
    return jnp.dot(a, b, preferred_element_type=F32)


def _cparams(sem):
    return pltpu.CompilerParams(dimension_semantics=sem, vmem_limit_bytes=VMEM_LIMIT)


def _resident(shape):
    zeros = (0,) * len(shape)
    return pl.BlockSpec(shape, lambda *_: zeros, pipeline_mode=pl.Buffered(1))


def _in_proj_kernel(x_ref, g_ref, w_ref, wu_ref, wlow_ref, wa2_ref, ba2_ref,
                    main_ref, u_ref, loga_ref, h_ref, uacc_ref, *, n_main):
    j = pl.program_id(1)

    @pl.when(j == 0)
    def _():
        x = x_ref[...]
        ms = jnp.mean(x * x, axis=-1, keepdims=True)
        h_ref[...] = (x * lax.rsqrt(ms + NORM_EPS) * g_ref[...]).astype(BF16)

    @pl.when(j < n_main)
    def _():
        main_ref[...] = _dot(h_ref[...], w_ref[...]).astype(BF16)

    @pl.when(j == n_main)
    def _():
        h = h_ref[...]
        a_low = _dot(h, wlow_ref[...])
        z = _dot(a_low, wa2_ref[...]) + ba2_ref[...]
        loga_ref[...] = _log_sigmoid(z) * (1.0 / GLA_GATE_TAU)
        uacc = _dot(h, wu_ref[...])
        rows, width = u_ref.shape[0], wu_ref.shape[1]
        for c in range(width // LANES):
            uacc_ref[c] = uacc[:, c * LANES:(c + 1) * LANES]
        for s in range(S5_STEPS):
            for c in range(width // LANES):
                lo = s * width + c * LANES
                u_ref[:, lo:lo + LANES] = uacc_ref[c, pl.ds(s, rows, stride=S5_STEPS), :].astype(BF16)


def _in_proj(x2, g, w_main, w_u, w_low, wa2p, ba2, *, tm, tn):
    n, d = x2.shape
    n_main = w_main.shape[1] // tn
    sw = w_u.shape[1]
    hk = wa2p.shape[1]
    kern = functools.partial(_in_proj_kernel, n_main=n_main)
    return pl.pallas_call(
        kern,
        grid=(n // tm, n_main + 1),
        in_specs=[
            pl.BlockSpec((tm, d), lambda i, j: (i, 0)),
            pl.BlockSpec((1, d), lambda i, j: (0, 0)),
            pl.BlockSpec((d, tn), lambda i, j: (0, jnp.minimum(j, n_main - 1))),
            _resident(w_u.shape),
            _resident(w_low.shape),
            _resident(wa2p.shape),
            _resident(ba2.shape),
        ],
        out_specs=[
            pl.BlockSpec((tm, tn), lambda i, j: (i, jnp.minimum(j, n_main - 1))),
            pl.BlockSpec((tm // S5_STEPS, S5_STEPS * sw), lambda i, j: (i, 0)),
            pl.BlockSpec((tm, hk), lambda i, j: (i, 0)),
        ],
        out_shape=[
            jax.ShapeDtypeStruct((n, n_main * tn), BF16),
            jax.ShapeDtypeStruct((n // S5_STEPS, S5_STEPS * sw), BF16),
            jax.ShapeDtypeStruct((n, hk), F32),
        ],
        scratch_shapes=[pltpu.VMEM((tm, d), BF16), pltpu.VMEM((sw // LANES, tm, LANES), F32)],
        compiler_params=_cparams(("parallel", "arbitrary")),
        name="in_proj",
    )(x2, g, w_main, w_u, w_low, wa2p, ba2)


def _kr_base(m):
    return GLA_CHUNK * (m * (m + 1) // 2)


def _gla_block(q_ref, k_ref, v_ref, loga_ref, ghn_ref, tri_ref,
               o_ref, s_ref, qd_ref, qb_ref, kb_ref, kr_ref, p_ref, *, n_sub, dk, dv):
    c = GLA_CHUNK

    log_a = loga_ref[...]
    la_hi = log_a.astype(BF16)
    la_lo = (log_a - la_hi.astype(F32)).astype(BF16)
    tri = tri_ref[...]
    bc_all = _dot(tri, la_hi) + _dot(tri, la_lo)

    bl = [bc_all[(i + 1) * c - 1:(i + 1) * c, :] for i in range(n_sub)]
    bs = [jnp.zeros_like(bl[0])]
    for i in range(n_sub):
        bs.append(bs[-1] + bl[i])
    btot = bs[n_sub]
    scale = dk ** -0.5

    for i in range(n_sub):
        rows = slice(i * c, (i + 1) * c)
        b = bc_all[rows, :]
        q = q_ref[rows, :].astype(F32) * scale
        k = k_ref[rows, :].astype(F32)
        qd = q * jnp.exp(b)
        ks = k * jnp.exp(bl[i] - b)
        qd_ref[rows, :] = qd.astype(BF16)
        qb_ref[rows, :] = (qd * jnp.exp(bs[i])).astype(BF16)
        kb_ref[rows, :] = (ks * jnp.exp(btot - bs[i + 1])).astype(BF16)
        own = _kr_base(i) + i * c
        kr_ref[own:own + c, :] = (k * jnp.exp(-b)).astype(BF16)
        for m in range(i + 1, n_sub):
            lo = _kr_base(m) + i * c
            kr_ref[lo:lo + c, :] = (ks * jnp.exp(bs[m] - bs[i + 1])).astype(BF16)

    row = lax.broadcasted_iota(jnp.int32, (c, c), 0)
    col = lax.broadcasted_iota(jnp.int32, (c, c), 1)
    causal = row >= col

    for h in range(GLA_HEADS):
        ks_ = slice(h * dk, (h + 1) * dk)
        vs_ = slice(h * dv, (h + 1) * dv)
        for m in range(n_sub):
            rows = slice(m * c, (m + 1) * c)
            keys = slice(_kr_base(m), _kr_base(m) + (m + 1) * c)
            s = lax.dot_general(qd_ref[rows, ks_], kr_ref[keys, ks_], (((1,), (1,)), ((), ())),
                                preferred_element_type=F32)
            if m > 0:
                p_ref[h, rows, 0:m * c] = s[:, 0:m * c].astype(BF16)
            p_ref[h, rows, m * c:(m + 1) * c] = jnp.where(causal, s[:, m * c:], 0.0).astype(BF16)

        v = v_ref[:, vs_]
        state = s_ref[h]
        o = _dot(p_ref[h], v) + _dot(qb_ref[:, ks_], state.astype(BF16))

        upd = lax.dot_general(kb_ref[:, ks_], v, (((0,), (0,)), ((), ())),
                              preferred_element_type=F32)
        decay_col = jnp.transpose(jnp.broadcast_to(jnp.exp(btot[:, ks_]), (dk, dk)))
        if dv > dk:
            decay_col = jnp.concatenate([decay_col] * (dv // dk), axis=1)
        s_ref[h] = decay_col * state + upd

        o = o * lax.rsqrt(jnp.mean(o * o, axis=-1, keepdims=True) + NORM_EPS)
        o_ref[:, vs_] = (o * ghn_ref[:, vs_]).astype(BF16)


def _gla_kernel(q_ref, k_ref, v_ref, loga_ref, ghn_ref, tri_ref,
                o_ref, s_ref, qd_ref, qb_ref, kb_ref, kr_ref, p_ref, *, tc, dk, dv):
    @pl.when(pl.program_id(1) == 0)
    def _():
        s_ref[...] = jnp.zeros_like(s_ref)
        p_ref[...] = jnp.zeros_like(p_ref)

    for blk in range(q_ref.shape[0] // tc):
        rows = pl.ds(blk * tc, tc)
        _gla_block(q_ref.at[rows], k_ref.at[rows], v_ref.at[rows], loga_ref.at[rows], ghn_ref, tri_ref,
                   o_ref.at[rows], s_ref, qd_ref, qb_ref, kb_ref, kr_ref, p_ref,
                   n_sub=tc // GLA_CHUNK, dk=dk, dv=dv)


def _gla(main3, loga3, ghn, tri, *, ts, tc, col_q, col_k, col_v, dk, dv):
    b, t, _ = main3.shape
    hk, hv = GLA_HEADS * dk, GLA_HEADS * dv
    kern = functools.partial(_gla_kernel, tc=tc, dk=dk, dv=dv)
    return pl.pallas_call(
        kern,
        grid=(b, t // ts),
        in_specs=[
            pl.BlockSpec((None, ts, hk), lambda i, j: (i, j, col_q // hk)),
            pl.BlockSpec((None, ts, hk), lambda i, j: (i, j, col_k // hk)),
            pl.BlockSpec((None, ts, hv), lambda i, j: (i, j, col_v // hv)),
            pl.BlockSpec((None, ts, hk), lambda i, j: (i, j, 0)),
            pl.BlockSpec((1, hv), lambda i, j: (0, 0)),
            pl.BlockSpec((tc, tc), lambda i, j: (0, 0)),
        ],
        out_specs=pl.BlockSpec((None, ts, hv), lambda i, j: (i, j, 0)),
        out_shape=jax.ShapeDtypeStruct((b, t, hv), BF16),
        scratch_shapes=[
            pltpu.VMEM((GLA_HEADS, dk, dv), F32),
            pltpu.VMEM((tc, hk), BF16),
            pltpu.VMEM((tc, hk), BF16),
            pltpu.VMEM((tc, hk), BF16),
            pltpu.VMEM((_kr_base(tc // GLA_CHUNK), hk), BF16),
            pltpu.VMEM((GLA_HEADS, tc, tc), BF16),
        ],
        compiler_params=_cparams(("parallel", "arbitrary")),
        name="gla",
    )(main3, main3, main3, loga3, ghn, tri)


def _cmul(ar, ai, br, bi):
    return ar * br - ai * bi, ar * bi + ai * br


def _s5_tables_kernel(lr_r, li_r, ldt_r, lr_c, li_c, ldt_c, bre, bim, cre, cim,
                      m_ref, wst_ref, wout_ref, pw_ref, *, gc, p):
    L = S5_STEPS
    sw = S5_SLAB_GROUPS * gc
    sp = S5_SLAB_GROUPS * p

    def discretise(lr, li, ldt):
        dt = jnp.exp(ldt)
        mag = jnp.exp(lr * dt)
        return mag * jnp.cos(li * dt), mag * jnp.sin(li * dt)

    lr, li = lr_r[...], li_r[...]
    ar, ai = discretise(lr, li, ldt_r[...])
    den = lr * lr + li * li
    am1 = ar - 1.0
    f_re = (am1 * lr + ai * li) / den
    f_im = (ai * lr - am1 * li) / den

    rg = lax.broadcasted_iota(jnp.int32, (sw, sp), 0) // gc
    lg = lax.broadcasted_iota(jnp.int32, (sw, sp), 1) // p
    bmask = rg == lg
    br, bi = bre[...], bim[...]
    bbr = jnp.where(bmask, f_re * br - f_im * bi, 0.0)
    bbi = jnp.where(bmask, f_re * bi + f_im * br, 0.0)

    rg2 = lax.broadcasted_iota(jnp.int32, (sp, sw), 0) // p
    lg2 = lax.broadcasted_iota(jnp.int32, (sp, sw), 1) // gc
    cmask = rg2 == lg2
    cr = jnp.where(cmask, cre[...], 0.0)
    ci = jnp.where(cmask, cim[...], 0.0)

    pr, pi = [jnp.ones_like(ar)], [jnp.zeros_like(ai)]
    for _ in range(L):
        nr, ni = _cmul(pr[-1], pi[-1], ar, ai)
        pr.append(nr)
        pi.append(ni)

    for j in range(L):
        xr, xi = _cmul(bbr, bbi, pr[L - 1 - j], pi[L - 1 - j])
        wst_ref[j * sw:(j + 1) * sw, 0:sp] = xr.astype(BF16)
        wst_ref[j * sw:(j + 1) * sw, sp:2 * sp] = xi.astype(BF16)

    kern = []
    for k in range(L):
        xr, xi = _cmul(bbr, bbi, pr[k], pi[k])
        kk = (jnp.dot(xr, cr, preferred_element_type=F32, precision=lax.Precision.HIGHEST)
              - jnp.dot(xi, ci, preferred_element_type=F32, precision=lax.Precision.HIGHEST))
        kern.append(kk.astype(BF16))
    zero = jnp.zeros((sw, sw), BF16)
    for j in range(L):
        for i in range(L):
            m_ref[j * sw:(j + 1) * sw, i * sw:(i + 1) * sw] = kern[i - j] if i >= j else zero

    acr, aci = discretise(lr_c[...], li_c[...], ldt_c[...])
    qr, qi = acr, aci
    for i in range(L):
        zr, zi = _cmul(cr, ci, qr, qi)
        wout_ref[0:sp, i * sw:(i + 1) * sw] = zr.astype(BF16)
        wout_ref[sp:2 * sp, i * sw:(i + 1) * sw] = (-zi).astype(BF16)
        qr, qi = _cmul(qr, qi, acr, aci)

    a8r, a8i = pr[L], pi[L]
    tr, ti = a8r, a8i
    for i in range(SUBLANES):
        pw_ref[i:i + 1, 0:sp] = tr
        pw_ref[i:i + 1, sp:2 * sp] = ti
        tr, ti = _cmul(tr, ti, a8r, a8i)


def _s5_tables(lam_re, lam_im, log_dt, b_re, b_im, c_re, c_im):
    g, p = lam_re.shape
    gc = b_re.shape[2]
    ns = g // S5_SLAB_GROUPS
    sw, sp = S5_SLAB_GROUPS * gc, S5_SLAB_GROUPS * p
    lw = S5_STEPS * sw

    def rows(a):
        return a.reshape(ns, 1, sp)

    def cols(a):
        return a.reshape(ns, sp, 1)

    ldt = jnp.broadcast_to(log_dt[:, None], (g, p))

    def b_layout(b):
        bt = jnp.transpose(b, (0, 2, 1)).reshape(ns, sw, p)
        return jnp.tile(bt, (1, 1, S5_SLAB_GROUPS))

    def c_layout(c):
        ct = jnp.transpose(c.reshape(ns, S5_SLAB_GROUPS, gc, p), (0, 3, 1, 2)).reshape(ns, p, sw)
        return jnp.tile(ct, (1, S5_SLAB_GROUPS, 1))

    row_spec = pl.BlockSpec((None, 1, sp), lambda s: (s, 0, 0))
    col_spec = pl.BlockSpec((None, sp, 1), lambda s: (s, 0, 0))
    b_spec = pl.BlockSpec((None, sw, sp), lambda s: (s, 0, 0))
    c_spec = pl.BlockSpec((None, sp, sw), lambda s: (s, 0, 0))
    kern = functools.partial(_s5_tables_kernel, gc=gc, p=p)
    return pl.pallas_call(
        kern,
        grid=(ns,),
        in_specs=[row_spec, row_spec, row_spec, col_spec, col_spec, col_spec,
                  b_spec, b_spec, c_spec, c_spec],
        out_specs=[
            pl.BlockSpec((None, lw, lw), lambda s: (s, 0, 0)),
            pl.BlockSpec((None, lw, 2 * sp), lambda s: (s, 0, 0)),
            pl.BlockSpec((None, 2 * sp, lw), lambda s: (s, 0, 0)),
            pl.BlockSpec((None, SUBLANES, 2 * sp), lambda s: (s, 0, 0)),
        ],
        out_shape=[
            jax.ShapeDtypeStruct((ns, lw, lw), BF16),
            jax.ShapeDtypeStruct((ns, lw, 2 * sp), BF16),
            jax.ShapeDtypeStruct((ns, 2 * sp, lw), BF16),
            jax.ShapeDtypeStruct((ns, SUBLANES, 2 * sp), F32),
        ],
        compiler_params=_cparams(("parallel",)),
        name="s5_tables",
    )(rows(lam_re), rows(lam_im), rows(ldt), cols(lam_re), cols(lam_im), cols(ldt),
      b_layout(b_re), b_layout(b_im), c_layout(c_re), c_layout(c_im))


def _s5_kernel(u_ref, m_ref, wst_ref, wout_ref, pw_ref, d_ref, wglu_ref, bglu_ref,
               o_ref, carry_ref, sloc_ref, sprev_ref, y_ref, out_ref, *, n_slabs, width):
    L = S5_STEPS
    sw = LANES
    lw = L * sw
    rb = u_ref.shape[0]
    sp2 = sloc_ref.shape[2]
    sp = sp2 // 2

    @pl.when(pl.program_id(1) == 0)
    def _():
        carry_ref[...] = jnp.zeros_like(carry_ref)

    row_id = lax.broadcasted_iota(jnp.int32, (SUBLANES, sp2), 0)

    def slab_input(s):
        return jnp.concatenate(
            [u_ref[:, j * width + s * sw: j * width + (s + 1) * sw] for j in range(L)], axis=1)

    for s in range(n_slabs):
        sloc_ref[s] = _dot(slab_input(s), wst_ref[s])

    pws = [pw_ref[s] for s in range(n_slabs)]
    carries = [carry_ref[s] for s in range(n_slabs)]
    for gi in range(rb // SUBLANES):
        grp = slice(gi * SUBLANES, (gi + 1) * SUBLANES)
        for s in range(n_slabs):
            pwr, pwi = pws[s][:, :sp], pws[s][:, sp:]
            carry = carries[s]
            x = sloc_ref[s, grp, :]
            for sh in (1, 2, 4):
                sft = jnp.where(row_id >= sh, pltpu.roll(x, sh, axis=0), 0.0)
                tr, ti = _cmul(sft[:, :sp], sft[:, sp:], pwr[sh - 1:sh, :], pwi[sh - 1:sh, :])
                x = x + jnp.concatenate([tr, ti], axis=1)
            cr, ci = _cmul(carry[:, :sp], carry[:, sp:], pwr, pwi)
            incl = x + jnp.concatenate([cr, ci], axis=1)
            sprev_ref[s, grp, :] = jnp.where(row_id == 0, carry, pltpu.roll(incl, 1, axis=0))
            carries[s] = incl[SUBLANES - 1:SUBLANES, :]
    for s in range(n_slabs):
        carry_ref[s] = carries[s]

    for s in range(n_slabs):
        u_slab = slab_input(s)
        y_inter = _dot(sprev_ref[s].astype(BF16), wout_ref[s])
        for nt in range(lw // MXU_TILE):
            hi = (nt + 1) * MXU_TILE
            cols = slice(nt * MXU_TILE, hi)
            y = y_inter[:, cols] + _dot(u_slab[:, 0:hi], m_ref[s, 0:hi, cols])
            for i in range(nt * MXU_TILE // sw, hi // sw):
                lo = i * sw - nt * MXU_TILE
                y_ref[:, i * width + s * sw: i * width + (s + 1) * sw] = y[:, lo:lo + sw]

    for i in range(L):
        cols = slice(i * width, (i + 1) * width)
        yi = y_ref[:, cols] + d_ref[...] * u_ref[:, cols].astype(F32)
        hh = _gelu_tanh(yi)
        gate = _sigmoid(_dot(hh.astype(BF16), wglu_ref[...]) + bglu_ref[...])
        val = hh * gate
        for c in range(width // LANES):
            out_ref[c, pl.ds(i, rb, stride=L), :] = val[:, c * LANES:(c + 1) * LANES]
    for c in range(width // LANES):
        o_ref[:, c * LANES:(c + 1) * LANES] = out_ref[c].astype(BF16)


def _s5(u3, m, wst, wout, pw, d_skip, wglu, bglu, *, rb):
    b, rows, lw = u3.shape
    ns = m.shape[0]
    width = lw // S5_STEPS
    sp2 = wst.shape[2]
    kern = functools.partial(_s5_kernel, n_slabs=ns, width=width)
    return pl.pallas_call(
        kern,
        grid=(b, rows // rb),
        in_specs=[
            pl.BlockSpec((None, rb, lw), lambda i, j: (i, j, 0)),
            _resident(m.shape), _resident(wst.shape), _resident(wout.shape), _resident(pw.shape),
            _resident(d_skip.shape), _resident(wglu.shape), _resident(bglu.shape),
        ],
        out_specs=pl.BlockSpec((None, rb * S5_STEPS, width), lambda i, j: (i, j, 0)),
        out_shape=jax.ShapeDtypeStruct((b, rows * S5_STEPS, width), BF16),
        scratch_shapes=[
            pltpu.VMEM((ns, 1, sp2), F32),
            pltpu.VMEM((ns, rb, sp2), F32),
            pltpu.VMEM((ns, rb, sp2), F32),
            pltpu.VMEM((rb, lw), F32),
            pltpu.VMEM((width // LANES, rb * S5_STEPS, LANES), F32),
        ],
        compiler_params=_cparams(("parallel", "arbitrary")),
        name="s5",
    )(u3, m, wst, wout, pw, d_skip, wglu, bglu)


def _mix_kernel(x_ref, ga_ref, gb_ref, r_ref, gla_ref, ob_ref, wa_ref, wb_ref, wo_ref, o_ref):
    r = r_ref[...].astype(F32)
    o_a = (r * _sigmoid(r) * gla_ref[...].astype(F32)).astype(BF16)
    a = _dot(o_a, wa_ref[...])
    b = _dot(ob_ref[...], wb_ref[...])
    mix = _sigmoid(ga_ref[...].astype(F32)) * a + _sigmoid(gb_ref[...].astype(F32)) * b
    o_ref[...] = x_ref[...] + _dot(mix.astype(BF16), wo_ref[...])


def _mix(x2, main, gla, o_b, w_a, w_b, w_o, *, tm, col_r):
    n, d = x2.shape
    hv = gla.shape[1]
    return pl.pallas_call(
        _mix_kernel,
        grid=(n // tm,),
        in_specs=[
            pl.BlockSpec((tm, d), lambda i: (i, 0)),
            pl.BlockSpec((tm, d), lambda i: (i, 0)),
            pl.BlockSpec((tm, d), lambda i: (i, 1)),
            pl.BlockSpec((tm, hv), lambda i: (i, col_r // hv)),
            pl.BlockSpec((tm, hv), lambda i: (i, 0)),
            pl.BlockSpec((tm, o_b.shape[1]), lambda i: (i, 0)),
            _resident(w_a.shape), _resident(w_b.shape), _resident(w_o.shape),
        ],
        out_specs=pl.BlockSpec((tm, d), lambda i: (i, 0)),
        out_shape=jax.ShapeDtypeStruct((n, d), F32),
        compiler_params=_cparams(("parallel",)),
        name="mix",
    )(x2, main, main, main, gla, o_b, w_a, w_b, w_o)


def _ffn_kernel(x_ref, g2_ref, wg_ref, wu_ref, wd_ref, gf_ref, o_ref, h_ref):
    j = pl.program_id(1)

    @pl.when(j == 0)
    def _():
        x = x_ref[...]
        ms = jnp.mean(x * x, axis=-1, keepdims=True)
        h_ref[...] = (x * lax.rsqrt(ms + NORM_EPS) * g2_ref[...]).astype(BF16)
        o_ref[...] = x

    h = h_ref[...]
    gate = _dot(h, wg_ref[...])
    up = _dot(h, wu_ref[...])
    act = (gate * _sigmoid(gate) * up).astype(BF16)
    o_ref[...] += _dot(act, wd_ref[...])

    @pl.when(j == pl.num_programs(1) - 1)
    def _():
        y = o_ref[...]
        ms = jnp.mean(y * y, axis=-1, keepdims=True)
        o_ref[...] = y * lax.rsqrt(ms + NORM_EPS) * gf_ref[...]


def _ffn(x1, g2, w_in, w_out, gf, *, tm, tf):
    n, d = x1.shape
    f = w_out.shape[0]
    nf = f // tf
    return pl.pallas_call(
        _ffn_kernel,
        grid=(n // tm, nf),
        in_specs=[
            pl.BlockSpec((tm, d), lambda i, j: (i, 0)),
            pl.BlockSpec((1, d), lambda i, j: (0, 0)),
            pl.BlockSpec((d, tf), lambda i, j: (0, j)),
            pl.BlockSpec((d, tf), lambda i, j: (0, j + nf)),
            pl.BlockSpec((tf, d), lambda i, j: (j, 0)),
            pl.BlockSpec((1, d), lambda i, j: (0, 0)),
        ],
        out_specs=pl.BlockSpec((tm, d), lambda i, j: (i, 0)),
        out_shape=jax.ShapeDtypeStruct((n, d), F32),
        scratch_shapes=[pltpu.VMEM((tm, d), BF16)],
        compiler_params=_cparams(("parallel", "arbitrary")),
        name="ffn",
    )(x1, g2, w_in, w_in, w_out, gf)


def _tile(n, pref):
    t = min(n, pref)
    assert n % t == 0, (n, t)
    return t


def _layer(x2, bsz, seq, norm1_g, w_in, w_a2, b_a2, gla_norm_g, lam_re, lam_im, log_dt,
           s5_b_re, s5_b_im, s5_c_re, s5_c_im, s5_d, w_glu, b_glu,
           w_branch_a, w_branch_b, w_out, norm2_g, w_ffn_in, w_ffn_out, final_g):
    n, d = x2.shape
    hk = w_a2.shape[1]
    hv = gla_norm_g.shape[0]
    rank = w_a2.shape[0]
    sw = w_glu.shape[0]
    dk, dv = hk // GLA_HEADS, hv // GLA_HEADS

    o_q, o_k, o_v, o_r = 0, hk, 2 * hk, 2 * hk + hv
    o_al = 2 * hk + 2 * hv
    o_u = o_al + rank
    o_ga = o_u + sw
    o_gb = o_ga + d
    w_main = jnp.concatenate(
        [w_in[:, o_ga:o_ga + d], w_in[:, o_gb:o_gb + d], w_in[:, o_v:o_v + hv], w_in[:, o_r:o_r + hv],
         w_in[:, o_q:o_q + hk], w_in[:, o_k:o_k + hk]], axis=1).astype(BF16)
    w_u = w_in[:, o_u:o_u + sw].astype(BF16)
    w_low = jnp.pad(w_in[:, o_al:o_al + rank], ((0, 0), (0, LANES - rank))).astype(BF16)
    wa2p = jnp.pad(w_a2, ((0, LANES - rank), (0, 0)))
    col_v, col_r, col_q, col_k = 2 * d, 2 * d + hv, 2 * d + 2 * hv, 2 * d + 2 * hv + hk

    main, u, loga = _in_proj(x2, norm1_g.reshape(1, d), w_main, w_u, w_low, wa2p, b_a2.reshape(1, hk),
                             tm=_tile(n, 1024), tn=1024)

    ts = _tile(seq, 512)
    tc = _tile(ts, 256)
    blk = jnp.arange(tc) // GLA_CHUNK
    tri = ((blk[:, None] == blk[None, :]) & (jnp.arange(tc)[:, None] >= jnp.arange(tc)[None, :]))
    gla = _gla(main.reshape(bsz, seq, -1), loga.reshape(bsz, seq, hk), gla_norm_g.reshape(1, hv),
               tri.astype(BF16), ts=ts, tc=tc, col_q=col_q, col_k=col_k, col_v=col_v, dk=dk, dv=dv)

    m, wst, wout, pw = _s5_tables(lam_re, lam_im, log_dt, s5_b_re, s5_b_im, s5_c_re, s5_c_im)
    rows = seq // S5_STEPS
    o_b = _s5(u.reshape(bsz, rows, S5_STEPS * sw), m, wst, wout, pw, s5_d.reshape(1, sw),
              w_glu.astype(BF16), b_glu.reshape(1, sw), rb=_tile(rows, 256))

    x1 = _mix(x2, main, gla.reshape(n, hv), o_b.reshape(n, sw), w_branch_a.astype(BF16),
              w_branch_b.astype(BF16), w_out.astype(BF16), tm=_tile(n, 512), col_r=col_r)
    return _ffn(x1, norm2_g.reshape(1, d), w_ffn_in.astype(BF16), w_ffn_out.astype(BF16),
                final_g.reshape(1, d), tm=_tile(n, 1024), tf=512)


def kernel(x, norm1_g, w_in, w_a2, b_a2, gla_norm_g, lam_re, lam_im, log_dt, s5_b_re, s5_b_im,
           s5_c_re, s5_c_im, s5_d, w_glu, b_glu, w_branch_a, w_branch_b, w_out, norm2_g,
           w_ffn_in, w_ffn_out, final_norm_g):
    bsz, seq, d = x.shape
    assert norm1_g.shape[0] == 1, "single-layer block"
    out = _layer(x.reshape(bsz * seq, d), bsz, seq, norm1_g[0], w_in[0], w_a2[0], b_a2[0],
                 gla_norm_g[0], lam_re[0], lam_im[0], log_dt[0], s5_b_re[0], s5_b_im[0],
                 s5_c_re[0], s5_c_im[0], s5_d[0], w_glu[0], b_glu[0], w_branch_a[0],
                 w_branch_b[0], w_out[0], norm2_g[0], w_ffn_in[0], w_ffn_out[0], final_norm_g[0])
    return out.reshape(bsz, seq, d)
```

```python
import functools
import math

import jax
import jax.numpy as jnp
from jax import lax
from jax.experimental import pallas as pl
from jax.experimental.pallas import tpu as pltpu

F32 = jnp.float32
BF16 = jnp.bfloat16

NORM_EPS = 1e-6
GLA_HEADS = 4
GLA_CHUNK = 32
GLA_GATE_TAU = 16.0
LANES = 128
SUBLANES = 8
MXU_TILE = 256
S5_STEPS = 8
S5_SLAB_GROUPS = 8
VMEM_LIMIT = 56 * 1024 * 1024


def _sigmoid(x):
    return 1.0 / (1.0 + jnp.exp(-x))


def _log_sigmoid(x):
    return jnp.minimum(x, 0.0) - jnp.log(1.0 + jnp.exp(-jnp.abs(x)))


def _gelu_tanh(x):
    c = math.sqrt(2.0 / math.pi)
    return 0.5 * x * (1.0 + jnp.tanh(c * (x + 0.044715 * (x * x * x))))


def _dot(a, b):
    return jnp.dot(a, b, preferred_element_type=F32)


def _cparams(sem):
    return pltpu.CompilerParams(dimension_semantics=sem, vmem_limit_bytes=VMEM_LIMIT)


def _resident(shape):
    zeros = (0,) * len(shape)
    return pl.BlockSpec(shape, lambda *_: zeros, pipeline_mode=pl.Buffered(1))


def _w_in_prep_kernel(w_ref, main_ref, u_ref, low_ref, *, segments, u_lo, u_w, low_lo, rank):
    dst = 0
    for lo, width in segments:
        main_ref[:, dst:dst + width] = w_ref[:, lo:lo + width].astype(BF16)
        dst += width
    u_ref[...] = w_ref[:, u_lo:u_lo + u_w].astype(BF16)
    lane = lax.broadcasted_iota(jnp.int32, low_ref.shape, 1)
    low_ref[...] = jnp.where(lane < rank, w_ref[:, low_lo:low_lo + LANES], 0.0).astype(BF16)


def _w_in_prep(w_in, *, segments, u_lo, u_w, low_lo, rank, tr):
    d, n_cols = w_in.shape
    n_main = sum(w for _, w in segments)
    kern = functools.partial(_w_in_prep_kernel, segments=segments, u_lo=u_lo, u_w=u_w,
                             low_lo=low_lo, rank=rank)
    return pl.pallas_call(
        kern,
        grid=(d // tr,),
        in_specs=[pl.BlockSpec((tr, n_cols), lambda i: (i, 0))],
        out_specs=[
            pl.BlockSpec((tr, n_main), lambda i: (i, 0)),
            pl.BlockSpec((tr, u_w), lambda i: (i, 0)),
            pl.BlockSpec((tr, LANES), lambda i: (i, 0)),
        ],
        out_shape=[
            jax.ShapeDtypeStruct((d, n_main), BF16),
            jax.ShapeDtypeStruct((d, u_w), BF16),
            jax.ShapeDtypeStruct((d, LANES), BF16),
        ],
        compiler_params=_cparams(("parallel",)),
        name="w_in_prep",
    )(w_in)


def _in_proj_kernel(x_ref, g_ref, w_ref, wu_ref, wlow_ref, wa2_ref, ba2_ref,
                    main_ref, u_ref, loga_ref, h0_ref, h1_ref, uacc_ref, *, n_main):
    i = pl.program_id(0)
    j = pl.program_id(1)
    chunk = x_ref.shape[0]

    def normalise_chunk(h_ref):
        x = x_ref[...]
        ms = jnp.mean(x * x, axis=-1, keepdims=True)
        rows = pl.ds(pl.multiple_of(j * chunk, chunk), chunk)
        h_ref[rows, :] = (x * lax.rsqrt(ms + NORM_EPS) * g_ref[...]).astype(BF16)

    def tail(h):
        a_low = _dot(h, wlow_ref[...])
        z = _dot(a_low, wa2_ref[...]) + ba2_ref[...]
        loga_ref[...] = _log_sigmoid(z) * (1.0 / GLA_GATE_TAU)
        uacc = _dot(h, wu_ref[...])
        rows, width = u_ref.shape[0], wu_ref.shape[1]
        for c in range(width // LANES):
            uacc_ref[c] = uacc[:, c * LANES:(c + 1) * LANES]
        for s in range(S5_STEPS):
            for c in range(width // LANES):
                lo = s * width + c * LANES
                u_ref[:, lo:lo + LANES] = uacc_ref[c, pl.ds(s, rows, stride=S5_STEPS), :].astype(BF16)

    @pl.when(i == 0)
    def _():
        normalise_chunk(h0_ref)

    for parity, (h_write, h_read) in enumerate(((h0_ref, h1_ref), (h1_ref, h0_ref))):
        active = (i > 0) & (i % 2 == parity)

        @pl.when(active & (j < n_main))
        def _(h_write=h_write, h_read=h_read):
            normalise_chunk(h_write)
            main_ref[...] = _dot(h_read[...], w_ref[...]).astype(BF16)

        @pl.when(active & (j == n_main))
        def _(h_write=h_write, h_read=h_read):
            normalise_chunk(h_write)
            tail(h_read[...])


def _in_proj(x2, g, w_main, w_u, w_low, wa2p, ba2, *, tm, tn):
    n, d = x2.shape
    n_main = w_main.shape[1] // tn
    sw = w_u.shape[1]
    hk = wa2p.shape[1]
    kern = functools.partial(_in_proj_kernel, n_main=n_main)
    nt = n // tm
    n_steps = n_main + 1
    assert tm % n_steps == 0
    chunk = tm // n_steps

    def prev(i):
        return jnp.maximum(i - 1, 0)

    return pl.pallas_call(
        kern,
        grid=(nt + 1, n_steps),
        in_specs=[
            pl.BlockSpec((chunk, d), lambda i, j: (jnp.minimum(i, nt - 1) * n_steps + j, 0)),
            pl.BlockSpec((1, d), lambda i, j: (0, 0)),
            pl.BlockSpec((d, tn), lambda i, j: (0, jnp.minimum(j, n_main - 1))),
            _resident(w_u.shape),
            _resident(w_low.shape),
            _resident(wa2p.shape),
            _resident(ba2.shape),
        ],
        out_specs=[
            pl.BlockSpec((tm, tn), lambda i, j: (prev(i), jnp.where(i == 0, 0, jnp.minimum(j, n_main - 1)))),
            pl.BlockSpec((tm // S5_STEPS, S5_STEPS * sw), lambda i, j: (prev(i), 0)),
            pl.BlockSpec((tm, hk), lambda i, j: (prev(i), 0)),
        ],
        out_shape=[
            jax.ShapeDtypeStruct((n, n_main * tn), BF16),
            jax.ShapeDtypeStruct((n // S5_STEPS, S5_STEPS * sw), BF16),
            jax.ShapeDtypeStruct((n, hk), F32),
        ],
        scratch_shapes=[pltpu.VMEM((tm, d), BF16), pltpu.VMEM((tm, d), BF16),
                        pltpu.VMEM((sw // LANES, tm, LANES), F32)],
        compiler_params=_cparams(("arbitrary", "arbitrary")),
        name="in_proj",
    )(x2, g, w_main, w_u, w_low, wa2p, ba2)


def _kr_base(m):
    return GLA_CHUNK * (m * (m + 1) // 2)


def _gla_block(q_ref, k_ref, v_ref, loga_ref, ghn_ref, tri_ref,
               o_ref, s_ref, qd_ref, qb_ref, kb_ref, kr_ref, p_ref, *, n_sub, dk, dv):
    c = GLA_CHUNK

    log_a = loga_ref[...]
    la_hi = log_a.astype(BF16)
    la_lo = (log_a - la_hi.astype(F32)).astype(BF16)
    tri = tri_ref[...]
    bc_all = _dot(tri, la_hi) + _dot(tri, la_lo)

    bl = [bc_all[(i + 1) * c - 1:(i + 1) * c, :] for i in range(n_sub)]
    bs = [jnp.zeros_like(bl[0])]
    for i in range(n_sub):
        bs.append(bs[-1] + bl[i])
    btot = bs[n_sub]
    scale = dk ** -0.5

    for i in range(n_sub):
        rows = slice(i * c, (i + 1) * c)
        b = bc_all[rows, :]
        q = q_ref[rows, :].astype(F32) * scale
        k = k_ref[rows, :].astype(F32)
        qd = q * jnp.exp(b)
        ks = k * jnp.exp(bl[i] - b)
        qd_ref[rows, :] = qd.astype(BF16)
        qb_ref[rows, :] = (qd * jnp.exp(bs[i])).astype(BF16)
        kb_ref[rows, :] = (ks * jnp.exp(btot - bs[i + 1])).astype(BF16)
        own = _kr_base(i) + i * c
        kr_ref[own:own + c, :] = (k * jnp.exp(-b)).astype(BF16)
        for m in range(i + 1, n_sub):
            lo = _kr_base(m) + i * c
            kr_ref[lo:lo + c, :] = (ks * jnp.exp(bs[m] - bs[i + 1])).astype(BF16)

    row = lax.broadcasted_iota(jnp.int32, (c, c), 0)
    col = lax.broadcasted_iota(jnp.int32, (c, c), 1)
    causal = row >= col

    for h in range(GLA_HEADS):
        ks_ = slice(h * dk, (h + 1) * dk)
        vs_ = slice(h * dv, (h + 1) * dv)
        for m in range(n_sub):
            rows = slice(m * c, (m + 1) * c)
            keys = slice(_kr_base(m), _kr_base(m) + (m + 1) * c)
            s = lax.dot_general(qd_ref[rows, ks_], kr_ref[keys, ks_], (((1,), (1,)), ((), ())),
                                preferred_element_type=F32)
            if m > 0:
                p_ref[h, rows, 0:m * c] = s[:, 0:m * c].astype(BF16)
            p_ref[h, rows, m * c:(m + 1) * c] = jnp.where(causal, s[:, m * c:], 0.0).astype(BF16)

        v = v_ref[:, vs_]
        state = s_ref[h]
        o = _dot(p_ref[h], v) + _dot(qb_ref[:, ks_], state.astype(BF16))

        upd = lax.dot_general(kb_ref[:, ks_], v, (((0,), (0,)), ((), ())),
                              preferred_element_type=F32)
        decay_col = jnp.transpose(jnp.broadcast_to(jnp.exp(btot[:, ks_]), (dk, dk)))
        if dv > dk:
            decay_col = jnp.concatenate([decay_col] * (dv // dk), axis=1)
        s_ref[h] = decay_col * state + upd

        o = o * lax.rsqrt(jnp.mean(o * o, axis=-1, keepdims=True) + NORM_EPS)
        o_ref[:, vs_] = (o * ghn_ref[:, vs_]).astype(BF16)


def _gla_kernel(q_ref, k_ref, v_ref, loga_ref, ghn_ref, tri_ref,
                o_ref, s_ref, qd_ref, qb_ref, kb_ref, kr_ref, p_ref, *, tc, dk, dv):
    @pl.when(pl.program_id(1) == 0)
    def _():
        s_ref[...] = jnp.zeros_like(s_ref)
        p_ref[...] = jnp.zeros_like(p_ref)

    for blk in range(q_ref.shape[0] // tc):
        rows = pl.ds(blk * tc, tc)
        _gla_block(q_ref.at[rows], k_ref.at[rows], v_ref.at[rows], loga_ref.at[rows], ghn_ref, tri_ref,
                   o_ref.at[rows], s_ref, qd_ref, qb_ref, kb_ref, kr_ref, p_ref,
                   n_sub=tc // GLA_CHUNK, dk=dk, dv=dv)


def _gla(main3, loga3, ghn, tri, *, ts, tc, col_q, col_k, col_v, dk, dv):
    b, t, _ = main3.shape
    hk, hv = GLA_HEADS * dk, GLA_HEADS * dv
    kern = functools.partial(_gla_kernel, tc=tc, dk=dk, dv=dv)
    return pl.pallas_call(
        kern,
        grid=(b, t // ts),
        in_specs=[
            pl.BlockSpec((None, ts, hk), lambda i, j: (i, j, col_q // hk)),
            pl.BlockSpec((None, ts, hk), lambda i, j: (i, j, col_k // hk)),
            pl.BlockSpec((None, ts, hv), lambda i, j: (i, j, col_v // hv)),
            pl.BlockSpec((None, ts, hk), lambda i, j: (i, j, 0)),
            pl.BlockSpec((1, hv), lambda i, j: (0, 0)),
            pl.BlockSpec((tc, tc), lambda i, j: (0, 0)),
        ],
        out_specs=pl.BlockSpec((None, ts, hv), lambda i, j: (i, j, 0)),
        out_shape=jax.ShapeDtypeStruct((b, t, hv), BF16),
        scratch_shapes=[
            pltpu.VMEM((GLA_HEADS, dk, dv), F32),
            pltpu.VMEM((tc, hk), BF16),
            pltpu.VMEM((tc, hk), BF16),
            pltpu.VMEM((tc, hk), BF16),
            pltpu.VMEM((_kr_base(tc // GLA_CHUNK), hk), BF16),
            pltpu.VMEM((GLA_HEADS, tc, tc), BF16),
        ],
        compiler_params=_cparams(("parallel", "arbitrary")),
        name="gla",
    )(main3, main3, main3, loga3, ghn, tri)


def _cmul(ar, ai, br, bi):
    return ar * br - ai * bi, ar * bi + ai * br


def _s5_tables_kernel(lr_r, li_r, ldt_r, lr_c, li_c, ldt_c, bre, bim, cre, cim,
                      m_ref, wst_ref, wout_ref, pw_ref, *, gc, p):
    L = S5_STEPS
    sw = S5_SLAB_GROUPS * gc
    sp = S5_SLAB_GROUPS * p

    def discretise(lr, li, ldt):
        dt = jnp.exp(ldt)
        mag = jnp.exp(lr * dt)
        return mag * jnp.cos(li * dt), mag * jnp.sin(li * dt)

    lr, li = lr_r[...], li_r[...]
    ar, ai = discretise(lr, li, ldt_r[...])
    den = lr * lr + li * li
    am1 = ar - 1.0
    f_re = (am1 * lr + ai * li) / den
    f_im = (ai * lr - am1 * li) / den

    rg = lax.broadcasted_iota(jnp.int32, (sw, sp), 0) // gc
    lg = lax.broadcasted_iota(jnp.int32, (sw, sp), 1) // p
    bmask = rg == lg
    br, bi = bre[...], bim[...]
    bbr = jnp.where(bmask, f_re * br - f_im * bi, 0.0)
    bbi = jnp.where(bmask, f_re * bi + f_im * br, 0.0)

    rg2 = lax.broadcasted_iota(jnp.int32, (sp, sw), 0) // p
    lg2 = lax.broadcasted_iota(jnp.int32, (sp, sw), 1) // gc
    cmask = rg2 == lg2
    cr = jnp.where(cmask, cre[...], 0.0)
    ci = jnp.where(cmask, cim[...], 0.0)

    pr, pi = [jnp.ones_like(ar)], [jnp.zeros_like(ai)]
    for _ in range(L):
        nr, ni = _cmul(pr[-1], pi[-1], ar, ai)
        pr.append(nr)
        pi.append(ni)

    for j in range(L):
        xr, xi = _cmul(bbr, bbi, pr[L - 1 - j], pi[L - 1 - j])
        wst_ref[j * sw:(j + 1) * sw, 0:sp] = xr.astype(BF16)
        wst_ref[j * sw:(j + 1) * sw, sp:2 * sp] = xi.astype(BF16)

    kern = []
    for k in range(L):
        xr, xi = _cmul(bbr, bbi, pr[k], pi[k])
        kk = (jnp.dot(xr, cr, preferred_element_type=F32, precision=lax.Precision.HIGHEST)
              - jnp.dot(xi, ci, preferred_element_type=F32, precision=lax.Precision.HIGHEST))
        kern.append(kk.astype(BF16))
    zero = jnp.zeros((sw, sw), BF16)
    for j in range(L):
        for i in range(L):
            m_ref[j * sw:(j + 1) * sw, i * sw:(i + 1) * sw] = kern[i - j] if i >= j else zero

    acr, aci = discretise(lr_c[...], li_c[...], ldt_c[...])
    qr, qi = acr, aci
    for i in range(L):
        zr, zi = _cmul(cr, ci, qr, qi)
        wout_ref[0:sp, i * sw:(i + 1) * sw] = zr.astype(BF16)
        wout_ref[sp:2 * sp, i * sw:(i + 1) * sw] = (-zi).astype(BF16)
        qr, qi = _cmul(qr, qi, acr, aci)

    a8r, a8i = pr[L], pi[L]
    tr, ti = a8r, a8i
    for i in range(SUBLANES):
        pw_ref[i:i + 1, 0:sp] = tr
        pw_ref[i:i + 1, sp:2 * sp] = ti
        tr, ti = _cmul(tr, ti, a8r, a8i)


def _s5_tables(lam_re, lam_im, log_dt, b_re, b_im, c_re, c_im):
    g, p = lam_re.shape
    gc = b_re.shape[2]
    ns = g // S5_SLAB_GROUPS
    sw, sp = S5_SLAB_GROUPS * gc, S5_SLAB_GROUPS * p
    lw = S5_STEPS * sw

    def rows(a):
        return a.reshape(ns, 1, sp)

    def cols(a):
        return a.reshape(ns, sp, 1)

    ldt = jnp.broadcast_to(log_dt[:, None], (g, p))

    def b_layout(b):
        bt = jnp.transpose(b, (0, 2, 1)).reshape(ns, sw, p)
        return jnp.tile(bt, (1, 1, S5_SLAB_GROUPS))

    def c_layout(c):
        ct = jnp.transpose(c.reshape(ns, S5_SLAB_GROUPS, gc, p), (0, 3, 1, 2)).reshape(ns, p, sw)
        return jnp.tile(ct, (1, S5_SLAB_GROUPS, 1))

    row_spec = pl.BlockSpec((None, 1, sp), lambda s: (s, 0, 0))
    col_spec = pl.BlockSpec((None, sp, 1), lambda s: (s, 0, 0))
    b_spec = pl.BlockSpec((None, sw, sp), lambda s: (s, 0, 0))
    c_spec = pl.BlockSpec((None, sp, sw), lambda s: (s, 0, 0))
    kern = functools.partial(_s5_tables_kernel, gc=gc, p=p)
    return pl.pallas_call(
        kern,
        grid=(ns,),
        in_specs=[row_spec, row_spec, row_spec, col_spec, col_spec, col_spec,
                  b_spec, b_spec, c_spec, c_spec],
        out_specs=[
            pl.BlockSpec((None, lw, lw), lambda s: (s, 0, 0)),
            pl.BlockSpec((None, lw, 2 * sp), lambda s: (s, 0, 0)),
            pl.BlockSpec((None, 2 * sp, lw), lambda s: (s, 0, 0)),
            pl.BlockSpec((None, SUBLANES, 2 * sp), lambda s: (s, 0, 0)),
        ],
        out_shape=[
            jax.ShapeDtypeStruct((ns, lw, lw), BF16),
            jax.ShapeDtypeStruct((ns, lw, 2 * sp), BF16),
            jax.ShapeDtypeStruct((ns, 2 * sp, lw), BF16),
            jax.ShapeDtypeStruct((ns, SUBLANES, 2 * sp), F32),
        ],
        compiler_params=_cparams(("parallel",)),
        name="s5_tables",
    )(rows(lam_re), rows(lam_im), rows(ldt), cols(lam_re), cols(lam_im), cols(ldt),
      b_layout(b_re), b_layout(b_im), c_layout(c_re), c_layout(c_im))


def _s5_kernel(u_ref, m_ref, wst_ref, wout_ref, pw_ref, d_ref, wglu_ref, bglu_ref,
               o_ref, carry_ref, sloc_ref, sprev_ref, y_ref, out_ref, *, n_slabs, width):
    L = S5_STEPS
    sw = LANES
    lw = L * sw
    rb = u_ref.shape[0]
    sp2 = sloc_ref.shape[2]
    sp = sp2 // 2

    @pl.when(pl.program_id(1) == 0)
    def _():
        carry_ref[...] = jnp.zeros_like(carry_ref)

    row_id = lax.broadcasted_iota(jnp.int32, (SUBLANES, sp2), 0)

    def slab_input(s):
        return jnp.concatenate(
            [u_ref[:, j * width + s * sw: j * width + (s + 1) * sw] for j in range(L)], axis=1)

    for s in range(n_slabs):
        sloc_ref[s] = _dot(slab_input(s), wst_ref[s])

    pws = [pw_ref[s] for s in range(n_slabs)]
    carries = [carry_ref[s] for s in range(n_slabs)]
    for gi in range(rb // SUBLANES):
        grp = slice(gi * SUBLANES, (gi + 1) * SUBLANES)
        for s in range(n_slabs):
            pwr, pwi = pws[s][:, :sp], pws[s][:, sp:]
            carry = carries[s]
            x = sloc_ref[s, grp, :]
            for sh in (1, 2, 4):
                sft = jnp.where(row_id >= sh, pltpu.roll(x, sh, axis=0), 0.0)
                tr, ti = _cmul(sft[:, :sp], sft[:, sp:], pwr[sh - 1:sh, :], pwi[sh - 1:sh, :])
                x = x + jnp.concatenate([tr, ti], axis=1)
            cr, ci = _cmul(carry[:, :sp], carry[:, sp:], pwr, pwi)
            incl = x + jnp.concatenate([cr, ci], axis=1)
            sprev_ref[s, grp, :] = jnp.where(row_id == 0, carry, pltpu.roll(incl, 1, axis=0))
            carries[s] = incl[SUBLANES - 1:SUBLANES, :]
    for s in range(n_slabs):
        carry_ref[s] = carries[s]

    for s in range(n_slabs):
        u_slab = slab_input(s)
        y_inter = _dot(sprev_ref[s].astype(BF16), wout_ref[s])
        for nt in range(lw // MXU_TILE):
            hi = (nt + 1) * MXU_TILE
            cols = slice(nt * MXU_TILE, hi)
            y = y_inter[:, cols] + _dot(u_slab[:, 0:hi], m_ref[s, 0:hi, cols])
            for i in range(nt * MXU_TILE // sw, hi // sw):
                lo = i * sw - nt * MXU_TILE
                y_ref[:, i * width + s * sw: i * width + (s + 1) * sw] = y[:, lo:lo + sw]

    for i in range(L):
        cols = slice(i * width, (i + 1) * width)
        yi = y_ref[:, cols] + d_ref[...] * u_ref[:, cols].astype(F32)
        hh = _gelu_tanh(yi)
        gate = _sigmoid(_dot(hh.astype(BF16), wglu_ref[...]) + bglu_ref[...])
        val = hh * gate
        for c in range(width // LANES):
            out_ref[c, pl.ds(i, rb, stride=L), :] = val[:, c * LANES:(c + 1) * LANES]
    for c in range(width // LANES):
        o_ref[:, c * LANES:(c + 1) * LANES] = out_ref[c].astype(BF16)


def _s5(u3, m, wst, wout, pw, d_skip, wglu, bglu, *, rb):
    b, rows, lw = u3.shape
    ns = m.shape[0]
    width = lw // S5_STEPS
    sp2 = wst.shape[2]
    kern = functools.partial(_s5_kernel, n_slabs=ns, width=width)
    return pl.pallas_call(
        kern,
        grid=(b, rows // rb),
        in_specs=[
            pl.BlockSpec((None, rb, lw), lambda i, j: (i, j, 0)),
            _resident(m.shape), _resident(wst.shape), _resident(wout.shape), _resident(pw.shape),
            _resident(d_skip.shape), _resident(wglu.shape), _resident(bglu.shape),
        ],
        out_specs=pl.BlockSpec((None, rb * S5_STEPS, width), lambda i, j: (i, j, 0)),
        out_shape=jax.ShapeDtypeStruct((b, rows * S5_STEPS, width), BF16),
        scratch_shapes=[
            pltpu.VMEM((ns, 1, sp2), F32),
            pltpu.VMEM((ns, rb, sp2), F32),
            pltpu.VMEM((ns, rb, sp2), F32),
            pltpu.VMEM((rb, lw), F32),
            pltpu.VMEM((width // LANES, rb * S5_STEPS, LANES), F32),
        ],
        compiler_params=_cparams(("parallel", "arbitrary")),
        name="s5",
    )(u3, m, wst, wout, pw, d_skip, wglu, bglu)


def _mix_kernel(x_ref, ga_ref, gb_ref, r_ref, gla_ref, ob_ref, wa_ref, wb_ref, wo_ref, o_ref):
    r = r_ref[...].astype(F32)
    o_a = (r * _sigmoid(r) * gla_ref[...].astype(F32)).astype(BF16)
    a = _dot(o_a, wa_ref[...])
    b = _dot(ob_ref[...], wb_ref[...])
    mix = _sigmoid(ga_ref[...].astype(F32)) * a + _sigmoid(gb_ref[...].astype(F32)) * b
    o_ref[...] = x_ref[...] + _dot(mix.astype(BF16), wo_ref[...])


def _mix(x2, main, gla, o_b, w_a, w_b, w_o, *, tm, col_r):
    n, d = x2.shape
    hv = gla.shape[1]
    return pl.pallas_call(
        _mix_kernel,
        grid=(n // tm,),
        in_specs=[
            pl.BlockSpec((tm, d), lambda i: (i, 0)),
            pl.BlockSpec((tm, d), lambda i: (i, 0)),
            pl.BlockSpec((tm, d), lambda i: (i, 1)),
            pl.BlockSpec((tm, hv), lambda i: (i, col_r // hv)),
            pl.BlockSpec((tm, hv), lambda i: (i, 0)),
            pl.BlockSpec((tm, o_b.shape[1]), lambda i: (i, 0)),
            _resident(w_a.shape), _resident(w_b.shape), _resident(w_o.shape),
        ],
        out_specs=pl.BlockSpec((tm, d), lambda i: (i, 0)),
        out_shape=jax.ShapeDtypeStruct((n, d), F32),
        compiler_params=_cparams(("parallel",)),
        name="mix",
    )(x2, main, main, main, gla, o_b, w_a, w_b, w_o)


def _ffn_kernel(x_ref, g2_ref, wg_ref, wu_ref, wd_ref, gf_ref, o_ref, h_ref):
    j = pl.program_id(1)

    @pl.when(j == 0)
    def _():
        x = x_ref[...]
        ms = jnp.mean(x * x, axis=-1, keepdims=True)
        h_ref[...] = (x * lax.rsqrt(ms + NORM_EPS) * g2_ref[...]).astype(BF16)
        o_ref[...] = x

    h = h_ref[...]
    gate = _dot(h, wg_ref[...])
    up = _dot(h, wu_ref[...])
    act = (gate * _sigmoid(gate) * up).astype(BF16)
    o_ref[...] += _dot(act, wd_ref[...])

    @pl.when(j == pl.num_programs(1) - 1)
    def _():
        y = o_ref[...]
        ms = jnp.mean(y * y, axis=-1, keepdims=True)
        o_ref[...] = y * lax.rsqrt(ms + NORM_EPS) * gf_ref[...]


def _ffn(x1, g2, w_in, w_out, gf, *, tm, tf):
    n, d = x1.shape
    f = w_out.shape[0]
    nf = f // tf
    return pl.pallas_call(
        _ffn_kernel,
        grid=(n // tm, nf),
        in_specs=[
            pl.BlockSpec((tm, d), lambda i, j: (i, 0)),
            pl.BlockSpec((1, d), lambda i, j: (0, 0)),
            pl.BlockSpec((d, tf), lambda i, j: (0, j)),
            pl.BlockSpec((d, tf), lambda i, j: (0, j + nf)),
            pl.BlockSpec((tf, d), lambda i, j: (j, 0)),
            pl.BlockSpec((1, d), lambda i, j: (0, 0)),
        ],
        out_specs=pl.BlockSpec((tm, d), lambda i, j: (i, 0)),
        out_shape=jax.ShapeDtypeStruct((n, d), F32),
        scratch_shapes=[pltpu.VMEM((tm, d), BF16)],
        compiler_params=_cparams(("parallel", "arbitrary")),
        name="ffn",
    )(x1, g2, w_in, w_in, w_out, gf)


def _tile(n, pref):
    t = min(n, pref)
    assert n % t == 0, (n, t)
    return t


def _layer(x2, bsz, seq, norm1_g, w_in, w_a2, b_a2, gla_norm_g, lam_re, lam_im, log_dt,
           s5_b_re, s5_b_im, s5_c_re, s5_c_im, s5_d, w_glu, b_glu,
           w_branch_a, w_branch_b, w_out, norm2_g, w_ffn_in, w_ffn_out, final_g):
    n, d = x2.shape
    hk = w_a2.shape[1]
    hv = gla_norm_g.shape[0]
    rank = w_a2.shape[0]
    sw = w_glu.shape[0]
    dk, dv = hk // GLA_HEADS, hv // GLA_HEADS

    o_q, o_k, o_v, o_r = 0, hk, 2 * hk, 2 * hk + hv
    o_al = 2 * hk + 2 * hv
    o_u = o_al + rank
    o_ga = o_u + sw
    o_gb = o_ga + d
    assert o_gb + d == w_in.shape[1] and o_al % LANES == 0
    w_main, w_u, w_low = _w_in_prep(
        w_in, segments=((o_ga, 2 * d), (o_v, 2 * hv), (o_q, 2 * hk)), u_lo=o_u, u_w=sw,
        low_lo=o_al, rank=rank, tr=_tile(d, 256))
    wa2p = jnp.pad(w_a2, ((0, LANES - rank), (0, 0)))
    col_v, col_r, col_q, col_k = 2 * d, 2 * d + hv, 2 * d + 2 * hv, 2 * d + 2 * hv + hk

    main, u, loga = _in_proj(x2, norm1_g.reshape(1, d), w_main, w_u, w_low, wa2p, b_a2.reshape(1, hk),
                             tm=_tile(n, 1024), tn=1024)

    ts = _tile(seq, 512)
    tc = _tile(ts, 256)
    blk = jnp.arange(tc) // GLA_CHUNK
    tri = ((blk[:, None] == blk[None, :]) & (jnp.arange(tc)[:, None] >= jnp.arange(tc)[None, :]))
    gla = _gla(main.reshape(bsz, seq, -1), loga.reshape(bsz, seq, hk), gla_norm_g.reshape(1, hv),
               tri.astype(BF16), ts=ts, tc=tc, col_q=col_q, col_k=col_k, col_v=col_v, dk=dk, dv=dv)

    m, wst, wout, pw = _s5_tables(lam_re, lam_im, log_dt, s5_b_re, s5_b_im, s5_c_re, s5_c_im)
    rows = seq // S5_STEPS
    o_b = _s5(u.reshape(bsz, rows, S5_STEPS * sw), m, wst, wout, pw, s5_d.reshape(1, sw),
              w_glu.astype(BF16), b_glu.reshape(1, sw), rb=_tile(rows, 256))

    x1 = _mix(x2, main, gla.reshape(n, hv), o_b.reshape(n, sw), w_branch_a.astype(BF16),
              w_branch_b.astype(BF16), w_out.astype(BF16), tm=_tile(n, 512), col_r=col_r)
    return _ffn(x1, norm2_g.reshape(1, d), w_ffn_in.astype(BF16), w_ffn_out.astype(BF16),
                final_g.reshape(1, d), tm=_tile(n, 1024), tf=512)


def kernel(x, norm1_g, w_in, w_a2, b_a2, gla_norm_g, lam_re, lam_im, log_dt, s5_b_re, s5_b_im,
           s5_c_re, s5_c_im, s5_d, w_glu, b_glu, w_branch_a, w_branch_b, w_out, norm2_g,
           w_ffn_in, w_ffn_out, final_norm_g):
    bsz, seq, d = x.shape
    assert norm1_g.shape[0] == 1, "single-layer block"
    out = _layer(x.reshape(bsz * seq, d), bsz, seq, norm1_g[0], w_in[0], w_a2[0], b_a2[0],
                 gla_norm_g[0], lam_re[0], lam_im[0], log_dt[0], s5_b_re[0], s5_b_im[0],
                 s5_c_re[0], s5_c_im[0], s5_d[0], w_glu[0], b_glu[0], w_branch_a[0],
                 w_branch_b[0], w_out[0], norm2_g[0], w_ffn_in[0], w_ffn_out[0], final_norm_g[0])
    return out.reshape(bsz, seq, d)
```

```python
import functools
import math

import jax
import jax.numpy as jnp
from jax import lax
from jax.experimental import pallas as pl
from jax.experimental.pallas import tpu as pltpu

F32 = jnp.float32
BF16 = jnp.bfloat16

NORM_EPS = 1e-6
GLA_HEADS = 4
GLA_CHUNK = 32
GLA_GATE_TAU = 16.0
LANES = 128
SUBLANES = 8
MXU_TILE = 256
ROW_UNIT = 16
S5_STEPS = 8
S5_SLAB_GROUPS = 8
VMEM_LIMIT = 56 * 1024 * 1024


def _sigmoid(x):
    return 1.0 / (1.0 + jnp.exp(-x))


def _log_sigmoid(x):
    return jnp.minimum(x, 0.0) - jnp.log(1.0 + jnp.exp(-jnp.abs(x)))


def _gelu_tanh(x):
    c = math.sqrt(2.0 / math.pi)
    return 0.5 * x * (1.0 + jnp.tanh(c * (x + 0.044715 * (x * x * x))))


def _dot(a, b):
    return jnp.dot(a, b, preferred_element_type=F32)


def _dot_nt(a, b):
    return lax.dot_general(a, b, (((1,), (1,)), ((), ())), preferred_element_type=F32)


def _cparams(sem):
    return pltpu.CompilerParams(dimension_semantics=sem, vmem_limit_bytes=VMEM_LIMIT)


def _resident(shape):
    zeros = (0,) * len(shape)
    return pl.BlockSpec(shape, lambda *_: zeros, pipeline_mode=pl.Buffered(1))


def _w_in_prep_kernel(starts_ref, w_ref, wlow_ref, out_ref, low_ref, *, rank):
    del starts_ref
    out_ref[...] = w_ref[...].astype(BF16)

    @pl.when(pl.program_id(0) == 0)
    def _():
        row = lax.broadcasted_iota(jnp.int32, low_ref.shape, 0)
        low_ref[...] = jnp.where(row < rank, wlow_ref[...], 0.0).astype(BF16)


def _w_in_prep(wt, *, starts, low_lo, rank, tr):
    _, d = wt.shape
    assert all(s % ROW_UNIT == 0 for s in starts) and low_lo % ROW_UNIT == 0
    n_blocks = len(starts)
    grid_spec = pltpu.PrefetchScalarGridSpec(
        num_scalar_prefetch=1,
        grid=(n_blocks,),
        in_specs=[
            pl.BlockSpec((pl.Element(tr), pl.Element(d)), lambda i, st: (st[i] * ROW_UNIT, 0)),
            pl.BlockSpec((pl.Element(LANES), pl.Element(d)), lambda i, st: (low_lo, 0)),
        ],
        out_specs=[
            pl.BlockSpec((tr, d), lambda i, st: (i, 0)),
            pl.BlockSpec((LANES, d), lambda i, st: (0, 0)),
        ],
    )
    return pl.pallas_call(
        functools.partial(_w_in_prep_kernel, rank=rank),
        grid_spec=grid_spec,
        out_shape=[
            jax.ShapeDtypeStruct((n_blocks * tr, d), BF16),
            jax.ShapeDtypeStruct((LANES, d), BF16),
        ],
        compiler_params=_cparams(("arbitrary",)),
        name="w_in_prep",
    )(jnp.asarray([s // ROW_UNIT for s in starts], jnp.int32), wt, wt)


def _in_proj_kernel(x_ref, g_ref, w_ref, wu_ref, wlow_ref, wa2_ref, ba2_ref,
                    main_ref, u_ref, loga_ref, h0_ref, h1_ref, uacc_ref, *, n_main):
    i = pl.program_id(0)
    j = pl.program_id(1)
    chunk = x_ref.shape[0]

    def normalise_chunk(h_ref):
        x = x_ref[...]
        ms = jnp.mean(x * x, axis=-1, keepdims=True)
        rows = pl.ds(pl.multiple_of(j * chunk, chunk), chunk)
        h_ref[rows, :] = (x * lax.rsqrt(ms + NORM_EPS) * g_ref[...]).astype(BF16)

    def tail(h):
        a_low = _dot_nt(h, wlow_ref[...])
        z = _dot(a_low, wa2_ref[...]) + ba2_ref[...]
        loga_ref[...] = _log_sigmoid(z) * (1.0 / GLA_GATE_TAU)
        uacc = _dot_nt(h, wu_ref[...])
        rows, width = u_ref.shape[0], wu_ref.shape[0]
        for c in range(width // LANES):
            uacc_ref[c] = uacc[:, c * LANES:(c + 1) * LANES]
        for s in range(S5_STEPS):
            for c in range(width // LANES):
                lo = s * width + c * LANES
                u_ref[:, lo:lo + LANES] = uacc_ref[c, pl.ds(s, rows, stride=S5_STEPS), :].astype(BF16)

    @pl.when(i == 0)
    def _():
        normalise_chunk(h0_ref)

    for parity, (h_write, h_read) in enumerate(((h0_ref, h1_ref), (h1_ref, h0_ref))):
        active = (i > 0) & (i % 2 == parity)

        @pl.when(active & (j < n_main))
        def _(h_write=h_write, h_read=h_read):
            normalise_chunk(h_write)
            main_ref[...] = _dot_nt(h_read[...], w_ref[...]).astype(BF16)

        @pl.when(active & (j == n_main))
        def _(h_write=h_write, h_read=h_read):
            normalise_chunk(h_write)
            tail(h_read[...])


def _in_proj(x2, g, wt, wt_low, wa2p, ba2, *, tm, tn, sw):
    n, d = x2.shape
    n_main = (wt.shape[0] - sw) // tn
    hk = wa2p.shape[1]
    kern = functools.partial(_in_proj_kernel, n_main=n_main)
    nt = n // tm
    n_steps = n_main + 1
    assert tm % n_steps == 0
    chunk = tm // n_steps

    def prev(i):
        return jnp.maximum(i - 1, 0)

    return pl.pallas_call(
        kern,
        grid=(nt + 1, n_steps),
        in_specs=[
            pl.BlockSpec((chunk, d), lambda i, j: (jnp.minimum(i, nt - 1) * n_steps + j, 0)),
            pl.BlockSpec((1, d), lambda i, j: (0, 0)),
            pl.BlockSpec((tn, d), lambda i, j: (jnp.minimum(j, n_main - 1), 0)),
            pl.BlockSpec((sw, d), lambda i, j: (n_main * tn // sw, 0), pipeline_mode=pl.Buffered(1)),
            _resident(wt_low.shape),
            _resident(wa2p.shape),
            _resident(ba2.shape),
        ],
        out_specs=[
            pl.BlockSpec((tm, tn), lambda i, j: (prev(i), jnp.where(i == 0, 0, jnp.minimum(j, n_main - 1)))),
            pl.BlockSpec((tm // S5_STEPS, S5_STEPS * sw), lambda i, j: (prev(i), 0)),
            pl.BlockSpec((tm, hk), lambda i, j: (prev(i), 0)),
        ],
        out_shape=[
            jax.ShapeDtypeStruct((n, n_main * tn), BF16),
            jax.ShapeDtypeStruct((n // S5_STEPS, S5_STEPS * sw), BF16),
            jax.ShapeDtypeStruct((n, hk), F32),
        ],
        scratch_shapes=[pltpu.VMEM((tm, d), BF16), pltpu.VMEM((tm, d), BF16),
                        pltpu.VMEM((sw // LANES, tm, LANES), F32)],
        compiler_params=_cparams(("arbitrary", "arbitrary")),
        name="in_proj",
    )(x2, g, wt, wt, wt_low, wa2p, ba2)


def _kr_base(m):
    return GLA_CHUNK * (m * (m + 1) // 2)


def _gla_block(q_ref, k_ref, v_ref, loga_ref, ghn_ref, tri_ref,
               o_ref, s_ref, qd_ref, qb_ref, kb_ref, kr_ref, p_ref, *, n_sub, dk, dv):
    c = GLA_CHUNK

    log_a = loga_ref[...]
    la_hi = log_a.astype(BF16)
    la_lo = (log_a - la_hi.astype(F32)).astype(BF16)
    tri = tri_ref[...]
    bc_all = _dot(tri, la_hi) + _dot(tri, la_lo)

    bl = [bc_all[(i + 1) * c - 1:(i + 1) * c, :] for i in range(n_sub)]
    bs = [jnp.zeros_like(bl[0])]
    for i in range(n_sub):
        bs.append(bs[-1] + bl[i])
    btot = bs[n_sub]
    scale = dk ** -0.5

    for i in range(n_sub):
        rows = slice(i * c, (i + 1) * c)
        b = bc_all[rows, :]
        q = q_ref[rows, :].astype(F32) * scale
        k = k_ref[rows, :].astype(F32)
        qd = q * jnp.exp(b)
        ks = k * jnp.exp(bl[i] - b)
        qd_ref[rows, :] = qd.astype(BF16)
        qb_ref[rows, :] = (qd * jnp.exp(bs[i])).astype(BF16)
        kb_ref[rows, :] = (ks * jnp.exp(btot - bs[i + 1])).astype(BF16)
        own = _kr_base(i) + i * c
        kr_ref[own:own + c, :] = (k * jnp.exp(-b)).astype(BF16)
        for m in range(i + 1, n_sub):
            lo = _kr_base(m) + i * c
            kr_ref[lo:lo + c, :] = (ks * jnp.exp(bs[m] - bs[i + 1])).astype(BF16)

    row = lax.broadcasted_iota(jnp.int32, (c, c), 0)
    col = lax.broadcasted_iota(jnp.int32, (c, c), 1)
    causal = row >= col

    for h in range(GLA_HEADS):
        ks_ = slice(h * dk, (h + 1) * dk)
        vs_ = slice(h * dv, (h + 1) * dv)
        for m in range(n_sub):
            rows = slice(m * c, (m + 1) * c)
            keys = slice(_kr_base(m), _kr_base(m) + (m + 1) * c)
            s = lax.dot_general(qd_ref[rows, ks_], kr_ref[keys, ks_], (((1,), (1,)), ((), ())),
                                preferred_element_type=F32)
            if m > 0:
                p_ref[h, rows, 0:m * c] = s[:, 0:m * c].astype(BF16)
            p_ref[h, rows, m * c:(m + 1) * c] = jnp.where(causal, s[:, m * c:], 0.0).astype(BF16)

        v = v_ref[:, vs_]
        state = s_ref[h]
        o = _dot(p_ref[h], v) + _dot(qb_ref[:, ks_], state.astype(BF16))

        upd = lax.dot_general(kb_ref[:, ks_], v, (((0,), (0,)), ((), ())),
                              preferred_element_type=F32)
        decay_col = jnp.transpose(jnp.broadcast_to(jnp.exp(btot[:, ks_]), (dk, dk)))
        if dv > dk:
            decay_col = jnp.concatenate([decay_col] * (dv // dk), axis=1)
        s_ref[h] = decay_col * state + upd

        o = o * lax.rsqrt(jnp.mean(o * o, axis=-1, keepdims=True) + NORM_EPS)
        o_ref[:, vs_] = (o * ghn_ref[:, vs_]).astype(BF16)


def _gla_kernel(q_ref, k_ref, v_ref, loga_ref, ghn_ref, tri_ref,
                o_ref, s_ref, qd_ref, qb_ref, kb_ref, kr_ref, p_ref, *, tc, dk, dv):
    @pl.when(pl.program_id(1) == 0)
    def _():
        s_ref[...] = jnp.zeros_like(s_ref)
        p_ref[...] = jnp.zeros_like(p_ref)

    for blk in range(q_ref.shape[0] // tc):
        rows = pl.ds(blk * tc, tc)
        _gla_block(q_ref.at[rows], k_ref.at[rows], v_ref.at[rows], loga_ref.at[rows], ghn_ref, tri_ref,
                   o_ref.at[rows], s_ref, qd_ref, qb_ref, kb_ref, kr_ref, p_ref,
                   n_sub=tc // GLA_CHUNK, dk=dk, dv=dv)


def _gla(main3, loga3, ghn, tri, *, ts, tc, col_q, col_k, col_v, dk, dv):
    b, t, _ = main3.shape
    hk, hv = GLA_HEADS * dk, GLA_HEADS * dv
    kern = functools.partial(_gla_kernel, tc=tc, dk=dk, dv=dv)
    return pl.pallas_call(
        kern,
        grid=(b, t // ts),
        in_specs=[
            pl.BlockSpec((None, ts, hk), lambda i, j: (i, j, col_q // hk)),
            pl.BlockSpec((None, ts, hk), lambda i, j: (i, j, col_k // hk)),
            pl.BlockSpec((None, ts, hv), lambda i, j: (i, j, col_v // hv)),
            pl.BlockSpec((None, ts, hk), lambda i, j: (i, j, 0)),
            pl.BlockSpec((1, hv), lambda i, j: (0, 0)),
            pl.BlockSpec((tc, tc), lambda i, j: (0, 0)),
        ],
        out_specs=pl.BlockSpec((None, ts, hv), lambda i, j: (i, j, 0)),
        out_shape=jax.ShapeDtypeStruct((b, t, hv), BF16),
        scratch_shapes=[
            pltpu.VMEM((GLA_HEADS, dk, dv), F32),
            pltpu.VMEM((tc, hk), BF16),
            pltpu.VMEM((tc, hk), BF16),
            pltpu.VMEM((tc, hk), BF16),
            pltpu.VMEM((_kr_base(tc // GLA_CHUNK), hk), BF16),
            pltpu.VMEM((GLA_HEADS, tc, tc), BF16),
        ],
        compiler_params=_cparams(("parallel", "arbitrary")),
        name="gla",
    )(main3, main3, main3, loga3, ghn, tri)


def _cmul(ar, ai, br, bi):
    return ar * br - ai * bi, ar * bi + ai * br


def _s5_tables_kernel(lr_r, li_r, ldt_r, lr_c, li_c, ldt_c, bre, bim, cre, cim,
                      m_ref, wst_ref, wout_ref, pw_ref, *, gc, p):
    L = S5_STEPS
    sw = S5_SLAB_GROUPS * gc
    sp = S5_SLAB_GROUPS * p

    def discretise(lr, li, ldt):
        dt = jnp.exp(ldt)
        mag = jnp.exp(lr * dt)
        return mag * jnp.cos(li * dt), mag * jnp.sin(li * dt)

    lr, li = lr_r[...], li_r[...]
    ar, ai = discretise(lr, li, ldt_r[...])
    den = lr * lr + li * li
    am1 = ar - 1.0
    f_re = (am1 * lr + ai * li) / den
    f_im = (ai * lr - am1 * li) / den

    rg = lax.broadcasted_iota(jnp.int32, (sw, sp), 0) // gc
    lg = lax.broadcasted_iota(jnp.int32, (sw, sp), 1) // p
    bmask = rg == lg
    br, bi = bre[...], bim[...]
    bbr = jnp.where(bmask, f_re * br - f_im * bi, 0.0)
    bbi = jnp.where(bmask, f_re * bi + f_im * br, 0.0)

    rg2 = lax.broadcasted_iota(jnp.int32, (sp, sw), 0) // p
    lg2 = lax.broadcasted_iota(jnp.int32, (sp, sw), 1) // gc
    cmask = rg2 == lg2
    cr = jnp.where(cmask, cre[...], 0.0)
    ci = jnp.where(cmask, cim[...], 0.0)

    pr, pi = [jnp.ones_like(ar)], [jnp.zeros_like(ai)]
    for _ in range(L):
        nr, ni = _cmul(pr[-1], pi[-1], ar, ai)
        pr.append(nr)
        pi.append(ni)

    for j in range(L):
        xr, xi = _cmul(bbr, bbi, pr[L - 1 - j], pi[L - 1 - j])
        wst_ref[j * sw:(j + 1) * sw, 0:sp] = xr.astype(BF16)
        wst_ref[j * sw:(j + 1) * sw, sp:2 * sp] = xi.astype(BF16)

    kern = []
    for k in range(L):
        xr, xi = _cmul(bbr, bbi, pr[k], pi[k])
        kk = (jnp.dot(xr, cr, preferred_element_type=F32, precision=lax.Precision.HIGHEST)
              - jnp.dot(xi, ci, preferred_element_type=F32, precision=lax.Precision.HIGHEST))
        kern.append(kk.astype(BF16))
    zero = jnp.zeros((sw, sw), BF16)
    for j in range(L):
        for i in range(L):
            m_ref[j * sw:(j + 1) * sw, i * sw:(i + 1) * sw] = kern[i - j] if i >= j else zero

    acr, aci = discretise(lr_c[...], li_c[...], ldt_c[...])
    qr, qi = acr, aci
    for i in range(L):
        zr, zi = _cmul(cr, ci, qr, qi)
        wout_ref[0:sp, i * sw:(i + 1) * sw] = zr.astype(BF16)
        wout_ref[sp:2 * sp, i * sw:(i + 1) * sw] = (-zi).astype(BF16)
        qr, qi = _cmul(qr, qi, acr, aci)

    a8r, a8i = pr[L], pi[L]
    tr, ti = a8r, a8i
    for i in range(SUBLANES):
        pw_ref[i:i + 1, 0:sp] = tr
        pw_ref[i:i + 1, sp:2 * sp] = ti
        tr, ti = _cmul(tr, ti, a8r, a8i)


def _s5_tables(lam_re, lam_im, log_dt, b_re, b_im, c_re, c_im):
    g, p = lam_re.shape
    gc = b_re.shape[2]
    ns = g // S5_SLAB_GROUPS
    sw, sp = S5_SLAB_GROUPS * gc, S5_SLAB_GROUPS * p
    lw = S5_STEPS * sw

    def rows(a):
        return a.reshape(ns, 1, sp)

    def cols(a):
        return a.reshape(ns, sp, 1)

    ldt = jnp.broadcast_to(log_dt[:, None], (g, p))

    def b_layout(b):
        bt = jnp.transpose(b, (0, 2, 1)).reshape(ns, sw, p)
        return jnp.tile(bt, (1, 1, S5_SLAB_GROUPS))

    def c_layout(c):
        ct = jnp.transpose(c.reshape(ns, S5_SLAB_GROUPS, gc, p), (0, 3, 1, 2)).reshape(ns, p, sw)
        return jnp.tile(ct, (1, S5_SLAB_GROUPS, 1))

    row_spec = pl.BlockSpec((None, 1, sp), lambda s: (s, 0, 0))
    col_spec = pl.BlockSpec((None, sp, 1), lambda s: (s, 0, 0))
    b_spec = pl.BlockSpec((None, sw, sp), lambda s: (s, 0, 0))
    c_spec = pl.BlockSpec((None, sp, sw), lambda s: (s, 0, 0))
    kern = functools.partial(_s5_tables_kernel, gc=gc, p=p)
    return pl.pallas_call(
        kern,
        grid=(ns,),
        in_specs=[row_spec, row_spec, row_spec, col_spec, col_spec, col_spec,
                  b_spec, b_spec, c_spec, c_spec],
        out_specs=[
            pl.BlockSpec((None, lw, lw), lambda s: (s, 0, 0)),
            pl.BlockSpec((None, lw, 2 * sp), lambda s: (s, 0, 0)),
            pl.BlockSpec((None, 2 * sp, lw), lambda s: (s, 0, 0)),
            pl.BlockSpec((None, SUBLANES, 2 * sp), lambda s: (s, 0, 0)),
        ],
        out_shape=[
            jax.ShapeDtypeStruct((ns, lw, lw), BF16),
            jax.ShapeDtypeStruct((ns, lw, 2 * sp), BF16),
            jax.ShapeDtypeStruct((ns, 2 * sp, lw), BF16),
            jax.ShapeDtypeStruct((ns, SUBLANES, 2 * sp), F32),
        ],
        compiler_params=_cparams(("parallel",)),
        name="s5_tables",
    )(rows(lam_re), rows(lam_im), rows(ldt), cols(lam_re), cols(lam_im), cols(ldt),
      b_layout(b_re), b_layout(b_im), c_layout(c_re), c_layout(c_im))


def _s5_kernel(u_ref, m_ref, wst_ref, wout_ref, pw_ref, d_ref, wglu_ref, bglu_ref,
               o_ref, carry_ref, sloc_ref, sprev_ref, y_ref, out_ref, *, n_slabs, width):
    L = S5_STEPS
    sw = LANES
    lw = L * sw
    rb = u_ref.shape[0]
    sp2 = sloc_ref.shape[2]
    sp = sp2 // 2

    @pl.when(pl.program_id(1) == 0)
    def _():
        carry_ref[...] = jnp.zeros_like(carry_ref)

    row_id = lax.broadcasted_iota(jnp.int32, (SUBLANES, sp2), 0)

    def slab_input(s):
        return jnp.concatenate(
            [u_ref[:, j * width + s * sw: j * width + (s + 1) * sw] for j in range(L)], axis=1)

    for s in range(n_slabs):
        sloc_ref[s] = _dot(slab_input(s), wst_ref[s])

    pws = [pw_ref[s] for s in range(n_slabs)]
    carries = [carry_ref[s] for s in range(n_slabs)]
    for gi in range(rb // SUBLANES):
        grp = slice(gi * SUBLANES, (gi + 1) * SUBLANES)
        for s in range(n_slabs):
            pwr, pwi = pws[s][:, :sp], pws[s][:, sp:]
            carry = carries[s]
            x = sloc_ref[s, grp, :]
            for sh in (1, 2, 4):
                sft = jnp.where(row_id >= sh, pltpu.roll(x, sh, axis=0), 0.0)
                tr, ti = _cmul(sft[:, :sp], sft[:, sp:], pwr[sh - 1:sh, :], pwi[sh - 1:sh, :])
                x = x + jnp.concatenate([tr, ti], axis=1)
            cr, ci = _cmul(carry[:, :sp], carry[:, sp:], pwr, pwi)
            incl = x + jnp.concatenate([cr, ci], axis=1)
            sprev_ref[s, grp, :] = jnp.where(row_id == 0, carry, pltpu.roll(incl, 1, axis=0))
            carries[s] = incl[SUBLANES - 1:SUBLANES, :]
    for s in range(n_slabs):
        carry_ref[s] = carries[s]

    for s in range(n_slabs):
        u_slab = slab_input(s)
        y_inter = _dot(sprev_ref[s].astype(BF16), wout_ref[s])
        for nt in range(lw // MXU_TILE):
            hi = (nt + 1) * MXU_TILE
            cols = slice(nt * MXU_TILE, hi)
            y = y_inter[:, cols] + _dot(u_slab[:, 0:hi], m_ref[s, 0:hi, cols])
            for i in range(nt * MXU_TILE // sw, hi // sw):
                lo = i * sw - nt * MXU_TILE
                y_ref[:, i * width + s * sw: i * width + (s + 1) * sw] = y[:, lo:lo + sw]

    for i in range(L):
        cols = slice(i * width, (i + 1) * width)
        yi = y_ref[:, cols] + d_ref[...] * u_ref[:, cols].astype(F32)
        hh = _gelu_tanh(yi)
        gate = _sigmoid(_dot(hh.astype(BF16), wglu_ref[...]) + bglu_ref[...])
        val = hh * gate
        for c in range(width // LANES):
            out_ref[c, pl.ds(i, rb, stride=L), :] = val[:, c * LANES:(c + 1) * LANES]
    for c in range(width // LANES):
        o_ref[:, c * LANES:(c + 1) * LANES] = out_ref[c].astype(BF16)


def _s5(u3, m, wst, wout, pw, d_skip, wglu, bglu, *, rb):
    b, rows, lw = u3.shape
    ns = m.shape[0]
    width = lw // S5_STEPS
    sp2 = wst.shape[2]
    kern = functools.partial(_s5_kernel, n_slabs=ns, width=width)
    return pl.pallas_call(
        kern,
        grid=(b, rows // rb),
        in_specs=[
            pl.BlockSpec((None, rb, lw), lambda i, j: (i, j, 0)),
            _resident(m.shape), _resident(wst.shape), _resident(wout.shape), _resident(pw.shape),
            _resident(d_skip.shape), _resident(wglu.shape), _resident(bglu.shape),
        ],
        out_specs=pl.BlockSpec((None, rb * S5_STEPS, width), lambda i, j: (i, j, 0)),
        out_shape=jax.ShapeDtypeStruct((b, rows * S5_STEPS, width), BF16),
        scratch_shapes=[
            pltpu.VMEM((ns, 1, sp2), F32),
            pltpu.VMEM((ns, rb, sp2), F32),
            pltpu.VMEM((ns, rb, sp2), F32),
            pltpu.VMEM((rb, lw), F32),
            pltpu.VMEM((width // LANES, rb * S5_STEPS, LANES), F32),
        ],
        compiler_params=_cparams(("parallel", "arbitrary")),
        name="s5",
    )(u3, m, wst, wout, pw, d_skip, wglu, bglu)


def _mix_kernel(x_ref, ga_ref, gb_ref, r_ref, gla_ref, ob_ref, wa_ref, wb_ref, wo_ref, o_ref):
    r = r_ref[...].astype(F32)
    o_a = (r * _sigmoid(r) * gla_ref[...].astype(F32)).astype(BF16)
    a = _dot(o_a, wa_ref[...])
    b = _dot(ob_ref[...], wb_ref[...])
    mix = _sigmoid(ga_ref[...].astype(F32)) * a + _sigmoid(gb_ref[...].astype(F32)) * b
    o_ref[...] = x_ref[...] + _dot(mix.astype(BF16), wo_ref[...])


def _mix(x2, main, gla, o_b, w_a, w_b, w_o, *, tm, col_r):
    n, d = x2.shape
    hv = gla.shape[1]
    return pl.pallas_call(
        _mix_kernel,
        grid=(n // tm,),
        in_specs=[
            pl.BlockSpec((tm, d), lambda i: (i, 0)),
            pl.BlockSpec((tm, d), lambda i: (i, 0)),
            pl.BlockSpec((tm, d), lambda i: (i, 1)),
            pl.BlockSpec((tm, hv), lambda i: (i, col_r // hv)),
            pl.BlockSpec((tm, hv), lambda i: (i, 0)),
            pl.BlockSpec((tm, o_b.shape[1]), lambda i: (i, 0)),
            _resident(w_a.shape), _resident(w_b.shape), _resident(w_o.shape),
        ],
        out_specs=pl.BlockSpec((tm, d), lambda i: (i, 0)),
        out_shape=jax.ShapeDtypeStruct((n, d), F32),
        compiler_params=_cparams(("parallel",)),
        name="mix",
    )(x2, main, main, main, gla, o_b, w_a, w_b, w_o)


def _ffn_kernel(x_ref, g2_ref, wg_ref, wu_ref, wd_ref, gf_ref, o_ref, h_ref):
    j = pl.program_id(1)

    @pl.when(j == 0)
    def _():
        x = x_ref[...]
        ms = jnp.mean(x * x, axis=-1, keepdims=True)
        h_ref[...] = (x * lax.rsqrt(ms + NORM_EPS) * g2_ref[...]).astype(BF16)
        o_ref[...] = x

    h = h_ref[...]
    gate = _dot(h, wg_ref[...])
    up = _dot(h, wu_ref[...])
    act = (gate * _sigmoid(gate) * up).astype(BF16)
    o_ref[...] += _dot(act, wd_ref[...])

    @pl.when(j == pl.num_programs(1) - 1)
    def _():
        y = o_ref[...]
        ms = jnp.mean(y * y, axis=-1, keepdims=True)
        o_ref[...] = y * lax.rsqrt(ms + NORM_EPS) * gf_ref[...]


def _ffn(x1, g2, w_in, w_out, gf, *, tm, tf):
    n, d = x1.shape
    f = w_out.shape[0]
    nf = f // tf
    return pl.pallas_call(
        _ffn_kernel,
        grid=(n // tm, nf),
        in_specs=[
            pl.BlockSpec((tm, d), lambda i, j: (i, 0)),
            pl.BlockSpec((1, d), lambda i, j: (0, 0)),
            pl.BlockSpec((d, tf), lambda i, j: (0, j)),
            pl.BlockSpec((d, tf), lambda i, j: (0, j + nf)),
            pl.BlockSpec((tf, d), lambda i, j: (j, 0)),
            pl.BlockSpec((1, d), lambda i, j: (0, 0)),
        ],
        out_specs=pl.BlockSpec((tm, d), lambda i, j: (i, 0)),
        out_shape=jax.ShapeDtypeStruct((n, d), F32),
        scratch_shapes=[pltpu.VMEM((tm, d), BF16)],
        compiler_params=_cparams(("parallel", "arbitrary")),
        name="ffn",
    )(x1, g2, w_in, w_in, w_out, gf)


def _tile(n, pref):
    t = min(n, pref)
    assert n % t == 0, (n, t)
    return t


def _layer(x2, bsz, seq, norm1_g, w_in, w_a2, b_a2, gla_norm_g, lam_re, lam_im, log_dt,
           s5_b_re, s5_b_im, s5_c_re, s5_c_im, s5_d, w_glu, b_glu,
           w_branch_a, w_branch_b, w_out, norm2_g, w_ffn_in, w_ffn_out, final_g):
    n, d = x2.shape
    hk = w_a2.shape[1]
    hv = gla_norm_g.shape[0]
    rank = w_a2.shape[0]
    sw = w_glu.shape[0]
    dk, dv = hk // GLA_HEADS, hv // GLA_HEADS

    o_q, o_k, o_v, o_r = 0, hk, 2 * hk, 2 * hk + hv
    o_al = 2 * hk + 2 * hv
    o_u = o_al + rank
    o_ga = o_u + sw
    o_gb = o_ga + d
    assert o_gb + d == w_in.shape[1]
    tr = sw
    segments = ((o_ga, 2 * d), (o_v, 2 * hv), (o_q, 2 * hk), (o_u, sw))
    starts = [lo + k for lo, width in segments for k in range(0, width, tr)]
    wt, wt_low = _w_in_prep(jnp.swapaxes(w_in, 0, 1), starts=starts, low_lo=o_al, rank=rank, tr=tr)
    wa2p = jnp.pad(w_a2, ((0, LANES - rank), (0, 0)))
    col_v, col_r, col_q, col_k = 2 * d, 2 * d + hv, 2 * d + 2 * hv, 2 * d + 2 * hv + hk

    main, u, loga = _in_proj(x2, norm1_g.reshape(1, d), wt, wt_low, wa2p, b_a2.reshape(1, hk),
                             tm=_tile(n, 1024), tn=1024, sw=sw)

    ts = _tile(seq, 512)
    tc = _tile(ts, 256)
    blk = jnp.arange(tc) // GLA_CHUNK
    tri = ((blk[:, None] == blk[None, :]) & (jnp.arange(tc)[:, None] >= jnp.arange(tc)[None, :]))
    gla = _gla(main.reshape(bsz, seq, -1), loga.reshape(bsz, seq, hk), gla_norm_g.reshape(1, hv),
               tri.astype(BF16), ts=ts, tc=tc, col_q=col_q, col_k=col_k, col_v=col_v, dk=dk, dv=dv)

    m, wst, wout, pw = _s5_tables(lam_re, lam_im, log_dt, s5_b_re, s5_b_im, s5_c_re, s5_c_im)
    rows = seq // S5_STEPS
    o_b = _s5(u.reshape(bsz, rows, S5_STEPS * sw), m, wst, wout, pw, s5_d.reshape(1, sw),
              w_glu.astype(BF16), b_glu.reshape(1, sw), rb=_tile(rows, 256))

    x1 = _mix(x2, main, gla.reshape(n, hv), o_b.reshape(n, sw), w_branch_a.astype(BF16),
              w_branch_b.astype(BF16), w_out.astype(BF16), tm=_tile(n, 512), col_r=col_r)
    return _ffn(x1, norm2_g.reshape(1, d), w_ffn_in.astype(BF16), w_ffn_out.astype(BF16),
                final_g.reshape(1, d), tm=_tile(n, 1024), tf=512)


def kernel(x, norm1_g, w_in, w_a2, b_a2, gla_norm_g, lam_re, lam_im, log_dt, s5_b_re, s5_b_im,
           s5_c_re, s5_c_im, s5_d, w_glu, b_glu, w_branch_a, w_branch_b, w_out, norm2_g,
           w_ffn_in, w_ffn_out, final_norm_g):
    bsz, seq, d = x.shape
    assert norm1_g.shape[0] == 1, "single-layer block"
    out = _layer(x.reshape(bsz * seq, d), bsz, seq, norm1_g[0], w_in[0], w_a2[0], b_a2[0],
                 gla_norm_g[0], lam_re[0], lam_im[0], log_dt[0], s5_b_re[0], s5_b_im[0],
                 s5_c_re[0], s5_c_im[0], s5_d[0], w_glu[0], b_glu[0], w_branch_a[0],
                 w_branch_b[0], w_out[0], norm2_g[0], w_ffn_in[0], w_ffn_out[0], final_norm_g[0])
    return out.reshape(bsz, seq, d)
```

```python
import functools
import math

import jax
import jax.numpy as jnp
from jax import lax
from jax.experimental import pallas as pl
from jax.experimental.pallas import tpu as pltpu

F32 = jnp.float32
BF16 = jnp.bfloat16

NORM_EPS = 1e-6
GLA_HEADS = 4
GLA_CHUNK = 32
GLA_GATE_TAU = 16.0
LANES = 128
SUBLANES = 8
MXU_TILE = 256
ROW_UNIT = 16
S5_STEPS = 8
S5_SLAB_GROUPS = 8
VMEM_LIMIT = 56 * 1024 * 1024


def _sigmoid(x):
    return 1.0 / (1.0 + jnp.exp(-x))


def _log_sigmoid(x):
    return jnp.minimum(x, 0.0) - jnp.log(1.0 + jnp.exp(-jnp.abs(x)))


def _gelu_tanh(x):
    c = math.sqrt(2.0 / math.pi)
    return 0.5 * x * (1.0 + jnp.tanh(c * (x + 0.044715 * (x * x * x))))


def _dot(a, b):
    return jnp.dot(a, b, preferred_element_type=F32)


def _dot_nt(a, b):
    return lax.dot_general(a, b, (((1,), (1,)), ((), ())), preferred_element_type=F32)


def _cparams(sem):
    return pltpu.CompilerParams(dimension_semantics=sem, vmem_limit_bytes=VMEM_LIMIT)


def _resident(shape):
    zeros = (0,) * len(shape)
    return pl.BlockSpec(shape, lambda *_: zeros, pipeline_mode=pl.Buffered(1))


def _w_in_prep_kernel(starts_ref, w_ref, wlow_ref, out_ref, low_ref, *, rank):
    del starts_ref
    out_ref[...] = w_ref[...].astype(BF16)

    @pl.when(pl.program_id(0) == 0)
    def _():
        row = lax.broadcasted_iota(jnp.int32, low_ref.shape, 0)
        low_ref[...] = jnp.where(row < rank, wlow_ref[...], 0.0).astype(BF16)


def _w_in_prep(wt, *, starts, low_lo, rank, tr):
    _, d = wt.shape
    assert all(s % ROW_UNIT == 0 for s in starts) and low_lo % ROW_UNIT == 0
    n_blocks = len(starts)
    grid_spec = pltpu.PrefetchScalarGridSpec(
        num_scalar_prefetch=1,
        grid=(n_blocks,),
        in_specs=[
            pl.BlockSpec((pl.Element(tr), pl.Element(d)), lambda i, st: (st[i] * ROW_UNIT, 0)),
            pl.BlockSpec((pl.Element(LANES), pl.Element(d)), lambda i, st: (low_lo, 0)),
        ],
        out_specs=[
            pl.BlockSpec((tr, d), lambda i, st: (i, 0)),
            pl.BlockSpec((LANES, d), lambda i, st: (0, 0)),
        ],
    )
    return pl.pallas_call(
        functools.partial(_w_in_prep_kernel, rank=rank),
        grid_spec=grid_spec,
        out_shape=[
            jax.ShapeDtypeStruct((n_blocks * tr, d), BF16),
            jax.ShapeDtypeStruct((LANES, d), BF16),
        ],
        compiler_params=_cparams(("arbitrary",)),
        name="w_in_prep",
    )(jnp.asarray([s // ROW_UNIT for s in starts], jnp.int32), wt, wt)


def _in_proj_kernel(x_ref, g_ref, w_ref, wu_ref, wlow_ref, wa2_ref, ba2_ref,
                    main_ref, u_ref, loga_ref, h0_ref, h1_ref, uacc_ref, *, n_main):
    i = pl.program_id(0)
    j = pl.program_id(1)
    chunk = x_ref.shape[0]

    def normalise_chunk(h_ref):
        x = x_ref[...]
        ms = jnp.mean(x * x, axis=-1, keepdims=True)
        rows = pl.ds(pl.multiple_of(j * chunk, chunk), chunk)
        h_ref[rows, :] = (x * lax.rsqrt(ms + NORM_EPS) * g_ref[...]).astype(BF16)

    def tail(h):
        a_low = _dot_nt(h, wlow_ref[...])
        z = _dot(a_low, wa2_ref[...]) + ba2_ref[...]
        loga_ref[...] = _log_sigmoid(z) * (1.0 / GLA_GATE_TAU)
        uacc = _dot_nt(h, wu_ref[...])
        rows, width = u_ref.shape[0], wu_ref.shape[0]
        for c in range(width // LANES):
            uacc_ref[c] = uacc[:, c * LANES:(c + 1) * LANES]
        for s in range(S5_STEPS):
            for c in range(width // LANES):
                lo = s * width + c * LANES
                u_ref[:, lo:lo + LANES] = uacc_ref[c, pl.ds(s, rows, stride=S5_STEPS), :].astype(BF16)

    @pl.when(i == 0)
    def _():
        normalise_chunk(h0_ref)

    for parity, (h_write, h_read) in enumerate(((h0_ref, h1_ref), (h1_ref, h0_ref))):
        active = (i > 0) & (i % 2 == parity)

        @pl.when(active & (j < n_main))
        def _(h_write=h_write, h_read=h_read):
            normalise_chunk(h_write)
            main_ref[...] = _dot_nt(h_read[...], w_ref[...]).astype(BF16)

        @pl.when(active & (j == n_main))
        def _(h_write=h_write, h_read=h_read):
            normalise_chunk(h_write)
            tail(h_read[...])


def _in_proj(x2, g, wt, wt_low, wa2p, ba2, *, tm, tn, sw):
    n, d = x2.shape
    n_main = (wt.shape[0] - sw) // tn
    hk = wa2p.shape[1]
    kern = functools.partial(_in_proj_kernel, n_main=n_main)
    nt = n // tm
    n_steps = n_main + 1
    assert tm % n_steps == 0
    chunk = tm // n_steps

    def prev(i):
        return jnp.maximum(i - 1, 0)

    return pl.pallas_call(
        kern,
        grid=(nt + 1, n_steps),
        in_specs=[
            pl.BlockSpec((chunk, d), lambda i, j: (jnp.minimum(i, nt - 1) * n_steps + j, 0)),
            pl.BlockSpec((1, d), lambda i, j: (0, 0)),
            pl.BlockSpec((tn, d), lambda i, j: (jnp.minimum(j, n_main - 1), 0)),
            pl.BlockSpec((sw, d), lambda i, j: (n_main * tn // sw, 0), pipeline_mode=pl.Buffered(1)),
            _resident(wt_low.shape),
            _resident(wa2p.shape),
            _resident(ba2.shape),
        ],
        out_specs=[
            pl.BlockSpec((tm, tn), lambda i, j: (prev(i), jnp.where(i == 0, 0, jnp.minimum(j, n_main - 1)))),
            pl.BlockSpec((tm // S5_STEPS, S5_STEPS * sw), lambda i, j: (prev(i), 0)),
            pl.BlockSpec((tm, hk), lambda i, j: (prev(i), 0)),
        ],
        out_shape=[
            jax.ShapeDtypeStruct((n, n_main * tn), BF16),
            jax.ShapeDtypeStruct((n // S5_STEPS, S5_STEPS * sw), BF16),
            jax.ShapeDtypeStruct((n, hk), F32),
        ],
        scratch_shapes=[pltpu.VMEM((tm, d), BF16), pltpu.VMEM((tm, d), BF16),
                        pltpu.VMEM((sw // LANES, tm, LANES), F32)],
        compiler_params=_cparams(("arbitrary", "arbitrary")),
        name="in_proj",
    )(x2, g, wt, wt, wt_low, wa2p, ba2)


def _kr_base(m):
    return GLA_CHUNK * (m * (m + 1) // 2)


def _gla_block(q_ref, k_ref, v_ref, loga_ref, ghn_ref, tri_ref,
               o_ref, s_ref, qd_ref, qb_ref, kb_ref, kr_ref, p_ref, *, n_sub, dk, dv):
    c = GLA_CHUNK
    shared = {}

    def decays():
        log_a = loga_ref[...]
        la_hi = log_a.astype(BF16)
        la_lo = (log_a - la_hi.astype(F32)).astype(BF16)
        tri = tri_ref[...]
        bc_all = _dot(tri, la_hi) + _dot(tri, la_lo)
        bl = [bc_all[(i + 1) * c - 1:(i + 1) * c, :] for i in range(n_sub)]
        bs = [jnp.zeros_like(bl[0])]
        for i in range(n_sub):
            bs.append(bs[-1] + bl[i])
        shared.update(bc_all=bc_all, bl=bl, bs=bs)

    def operands(i):
        bc_all, bl, bs = shared["bc_all"], shared["bl"], shared["bs"]
        btot = bs[n_sub]
        rows = slice(i * c, (i + 1) * c)
        b = bc_all[rows, :]
        q = q_ref[rows, :].astype(F32) * (dk ** -0.5)
        k = k_ref[rows, :].astype(F32)
        qd = q * jnp.exp(b)
        ks = k * jnp.exp(bl[i] - b)
        qd_ref[rows, :] = qd.astype(BF16)
        qb_ref[rows, :] = (qd * jnp.exp(bs[i])).astype(BF16)
        kb_ref[rows, :] = (ks * jnp.exp(btot - bs[i + 1])).astype(BF16)
        own = _kr_base(i) + i * c
        kr_ref[own:own + c, :] = (k * jnp.exp(-b)).astype(BF16)
        for m in range(i + 1, n_sub):
            lo = _kr_base(m) + i * c
            kr_ref[lo:lo + c, :] = (ks * jnp.exp(bs[m] - bs[i + 1])).astype(BF16)

    def scores(h):
        row = lax.broadcasted_iota(jnp.int32, (c, c), 0)
        col = lax.broadcasted_iota(jnp.int32, (c, c), 1)
        causal = row >= col
        ks_ = slice(h * dk, (h + 1) * dk)
        for m in range(n_sub):
            rows = slice(m * c, (m + 1) * c)
            keys = slice(_kr_base(m), _kr_base(m) + (m + 1) * c)
            s = lax.dot_general(qd_ref[rows, ks_], kr_ref[keys, ks_], (((1,), (1,)), ((), ())),
                                preferred_element_type=F32)
            if m > 0:
                p_ref[h, rows, 0:m * c] = s[:, 0:m * c].astype(BF16)
            p_ref[h, rows, m * c:(m + 1) * c] = jnp.where(causal, s[:, m * c:], 0.0).astype(BF16)

    def head(h):
        btot = shared["bs"][n_sub]
        ks_ = slice(h * dk, (h + 1) * dk)
        vs_ = slice(h * dv, (h + 1) * dv)
        v = v_ref[:, vs_]
        state = s_ref[h]
        o = _dot(p_ref[h], v) + _dot(qb_ref[:, ks_], state.astype(BF16))

        upd = lax.dot_general(kb_ref[:, ks_], v, (((0,), (0,)), ((), ())),
                              preferred_element_type=F32)
        decay_col = jnp.transpose(jnp.broadcast_to(jnp.exp(btot[:, ks_]), (dk, dk)))
        if dv > dk:
            decay_col = jnp.concatenate([decay_col] * (dv // dk), axis=1)
        s_ref[h] = decay_col * state + upd
        shared["o", h] = o

    def finish(h):
        vs_ = slice(h * dv, (h + 1) * dv)
        o = shared["o", h]
        o = o * lax.rsqrt(jnp.mean(o * o, axis=-1, keepdims=True) + NORM_EPS)
        o_ref[:, vs_] = (o * ghn_ref[:, vs_]).astype(BF16)

    stages = [functools.partial(scores, 0), functools.partial(scores, 1)]
    for h in range(GLA_HEADS):
        if h + 2 < GLA_HEADS:
            stages.append(functools.partial(scores, h + 2))
        stages.append(functools.partial(head, h))
        if h > 0:
            stages.append(functools.partial(finish, h - 1))
    stages.append(functools.partial(finish, GLA_HEADS - 1))
    return decays, [functools.partial(operands, i) for i in range(n_sub)], stages


def _gla_kernel(q_ref, k_ref, v_ref, loga_ref, ghn_ref, tri_ref,
                o_ref, s_ref, qd_ref, qb_ref, kb_ref, kr_ref, p_ref, *, tc, dk, dv):
    @pl.when(pl.program_id(1) == 0)
    def _():
        s_ref[...] = jnp.zeros_like(s_ref)
        p_ref[...] = jnp.zeros_like(p_ref)

    blocks = []
    for blk in range(q_ref.shape[0] // tc):
        rows = pl.ds(blk * tc, tc)
        blocks.append(_gla_block(
            q_ref.at[rows], k_ref.at[rows], v_ref.at[rows], loga_ref.at[rows], ghn_ref, tri_ref,
            o_ref.at[rows], s_ref, qd_ref.at[blk], qb_ref.at[blk], kb_ref.at[blk], kr_ref.at[blk],
            p_ref.at[blk], n_sub=tc // GLA_CHUNK, dk=dk, dv=dv))

    decays, operands, _ = blocks[0]
    decays()
    for stage in operands:
        stage()
    for blk, (_, _, heads) in enumerate(blocks):
        nxt = blocks[blk + 1] if blk + 1 < len(blocks) else None
        if nxt is not None:
            nxt[0]()
        per_stage = -(-len(nxt[1]) // len(heads)) if nxt is not None else 0
        for h, stage in enumerate(heads):
            stage()
            if nxt is not None:
                for operand_stage in nxt[1][h * per_stage:(h + 1) * per_stage]:
                    operand_stage()


def _gla(main3, loga3, ghn, tri, *, ts, tc, col_q, col_k, col_v, dk, dv):
    b, t, _ = main3.shape
    hk, hv = GLA_HEADS * dk, GLA_HEADS * dv
    kern = functools.partial(_gla_kernel, tc=tc, dk=dk, dv=dv)
    nb = ts // tc
    return pl.pallas_call(
        kern,
        grid=(b, t // ts),
        in_specs=[
            pl.BlockSpec((None, ts, hk), lambda i, j: (i, j, col_q // hk)),
            pl.BlockSpec((None, ts, hk), lambda i, j: (i, j, col_k // hk)),
            pl.BlockSpec((None, ts, hv), lambda i, j: (i, j, col_v // hv)),
            pl.BlockSpec((None, ts, hk), lambda i, j: (i, j, 0)),
            pl.BlockSpec((1, hv), lambda i, j: (0, 0)),
            pl.BlockSpec((tc, tc), lambda i, j: (0, 0)),
        ],
        out_specs=pl.BlockSpec((None, ts, hv), lambda i, j: (i, j, 0)),
        out_shape=jax.ShapeDtypeStruct((b, t, hv), BF16),
        scratch_shapes=[
            pltpu.VMEM((GLA_HEADS, dk, dv), F32),
            pltpu.VMEM((nb, tc, hk), BF16),
            pltpu.VMEM((nb, tc, hk), BF16),
            pltpu.VMEM((nb, tc, hk), BF16),
            pltpu.VMEM((nb, _kr_base(tc // GLA_CHUNK), hk), BF16),
            pltpu.VMEM((nb, GLA_HEADS, tc, tc), BF16),
        ],
        compiler_params=_cparams(("parallel", "arbitrary")),
        name="gla",
    )(main3, main3, main3, loga3, ghn, tri)


def _cmul(ar, ai, br, bi):
    return ar * br - ai * bi, ar * bi + ai * br


def _s5_tables_kernel(lr_r, li_r, ldt_r, lr_c, li_c, ldt_c, bre, bim, cre, cim,
                      m_ref, wst_ref, wout_ref, pw_ref, *, gc, p):
    L = S5_STEPS
    sw = S5_SLAB_GROUPS * gc
    sp = S5_SLAB_GROUPS * p

    def discretise(lr, li, ldt):
        dt = jnp.exp(ldt)
        mag = jnp.exp(lr * dt)
        return mag * jnp.cos(li * dt), mag * jnp.sin(li * dt)

    lr, li = lr_r[...], li_r[...]
    ar, ai = discretise(lr, li, ldt_r[...])
    den = lr * lr + li * li
    am1 = ar - 1.0
    f_re = (am1 * lr + ai * li) / den
    f_im = (ai * lr - am1 * li) / den

    rg = lax.broadcasted_iota(jnp.int32, (sw, sp), 0) // gc
    lg = lax.broadcasted_iota(jnp.int32, (sw, sp), 1) // p
    bmask = rg == lg
    br, bi = bre[...], bim[...]
    bbr = jnp.where(bmask, f_re * br - f_im * bi, 0.0)
    bbi = jnp.where(bmask, f_re * bi + f_im * br, 0.0)

    rg2 = lax.broadcasted_iota(jnp.int32, (sp, sw), 0) // p
    lg2 = lax.broadcasted_iota(jnp.int32, (sp, sw), 1) // gc
    cmask = rg2 == lg2
    cr = jnp.where(cmask, cre[...], 0.0)
    ci = jnp.where(cmask, cim[...], 0.0)

    pr, pi = [jnp.ones_like(ar)], [jnp.zeros_like(ai)]
    for _ in range(L):
        nr, ni = _cmul(pr[-1], pi[-1], ar, ai)
        pr.append(nr)
        pi.append(ni)

    for j in range(L):
        xr, xi = _cmul(bbr, bbi, pr[L - 1 - j], pi[L - 1 - j])
        wst_ref[j * sw:(j + 1) * sw, 0:sp] = xr.astype(BF16)
        wst_ref[j * sw:(j + 1) * sw, sp:2 * sp] = xi.astype(BF16)

    kern = []
    for k in range(L):
        xr, xi = _cmul(bbr, bbi, pr[k], pi[k])
        kk = (jnp.dot(xr, cr, preferred_element_type=F32, precision=lax.Precision.HIGHEST)
              - jnp.dot(xi, ci, preferred_element_type=F32, precision=lax.Precision.HIGHEST))
        kern.append(kk.astype(BF16))
    zero = jnp.zeros((sw, sw), BF16)
    for j in range(L):
        for i in range(L):
            m_ref[j * sw:(j + 1) * sw, i * sw:(i + 1) * sw] = kern[i - j] if i >= j else zero

    acr, aci = discretise(lr_c[...], li_c[...], ldt_c[...])
    qr, qi = acr, aci
    for i in range(L):
        zr, zi = _cmul(cr, ci, qr, qi)
        wout_ref[0:sp, i * sw:(i + 1) * sw] = zr.astype(BF16)
        wout_ref[sp:2 * sp, i * sw:(i + 1) * sw] = (-zi).astype(BF16)
        qr, qi = _cmul(qr, qi, acr, aci)

    a8r, a8i = pr[L], pi[L]
    tr, ti = a8r, a8i
    for i in range(SUBLANES):
        pw_ref[i:i + 1, 0:sp] = tr
        pw_ref[i:i + 1, sp:2 * sp] = ti
        tr, ti = _cmul(tr, ti, a8r, a8i)


def _s5_tables(lam_re, lam_im, log_dt, b_re, b_im, c_re, c_im):
    g, p = lam_re.shape
    gc = b_re.shape[2]
    ns = g // S5_SLAB_GROUPS
    sw, sp = S5_SLAB_GROUPS * gc, S5_SLAB_GROUPS * p
    lw = S5_STEPS * sw

    def rows(a):
        return a.reshape(ns, 1, sp)

    def cols(a):
        return a.reshape(ns, sp, 1)

    ldt = jnp.broadcast_to(log_dt[:, None], (g, p))

    def b_layout(b):
        bt = jnp.transpose(b, (0, 2, 1)).reshape(ns, sw, p)
        return jnp.tile(bt, (1, 1, S5_SLAB_GROUPS))

    def c_layout(c):
        ct = jnp.transpose(c.reshape(ns, S5_SLAB_GROUPS, gc, p), (0, 3, 1, 2)).reshape(ns, p, sw)
        return jnp.tile(ct, (1, S5_SLAB_GROUPS, 1))

    row_spec = pl.BlockSpec((None, 1, sp), lambda s: (s, 0, 0))
    col_spec = pl.BlockSpec((None, sp, 1), lambda s: (s, 0, 0))
    b_spec = pl.BlockSpec((None, sw, sp), lambda s: (s, 0, 0))
    c_spec = pl.BlockSpec((None, sp, sw), lambda s: (s, 0, 0))
    kern = functools.partial(_s5_tables_kernel, gc=gc, p=p)
    return pl.pallas_call(
        kern,
        grid=(ns,),
        in_specs=[row_spec, row_spec, row_spec, col_spec, col_spec, col_spec,
                  b_spec, b_spec, c_spec, c_spec],
        out_specs=[
            pl.BlockSpec((None, lw, lw), lambda s: (s, 0, 0)),
            pl.BlockSpec((None, lw, 2 * sp), lambda s: (s, 0, 0)),
            pl.BlockSpec((None, 2 * sp, lw), lambda s: (s, 0, 0)),
            pl.BlockSpec((None, SUBLANES, 2 * sp), lambda s: (s, 0, 0)),
        ],
        out_shape=[
            jax.ShapeDtypeStruct((ns, lw, lw), BF16),
            jax.ShapeDtypeStruct((ns, lw, 2 * sp), BF16),
            jax.ShapeDtypeStruct((ns, 2 * sp, lw), BF16),
            jax.ShapeDtypeStruct((ns, SUBLANES, 2 * sp), F32),
        ],
        compiler_params=_cparams(("parallel",)),
        name="s5_tables",
    )(rows(lam_re), rows(lam_im), rows(ldt), cols(lam_re), cols(lam_im), cols(ldt),
      b_layout(b_re), b_layout(b_im), c_layout(c_re), c_layout(c_im))


def _s5_kernel(u_ref, m_ref, wst_ref, wout_ref, pw_ref, d_ref, wglu_ref, bglu_ref,
               o_ref, carry_ref, sloc_ref, sprev_ref, y_ref, out_ref, *, n_slabs, width):
    L = S5_STEPS
    sw = LANES
    lw = L * sw
    rb = u_ref.shape[0]
    sp2 = sloc_ref.shape[2]
    sp = sp2 // 2

    @pl.when(pl.program_id(1) == 0)
    def _():
        carry_ref[...] = jnp.zeros_like(carry_ref)

    row_id = lax.broadcasted_iota(jnp.int32, (SUBLANES, sp2), 0)

    def slab_input(s):
        return jnp.concatenate(
            [u_ref[:, j * width + s * sw: j * width + (s + 1) * sw] for j in range(L)], axis=1)

    for s in range(n_slabs):
        sloc_ref[s] = _dot(slab_input(s), wst_ref[s])

    pws = [pw_ref[s] for s in range(n_slabs)]
    carries = [carry_ref[s] for s in range(n_slabs)]
    for gi in range(rb // SUBLANES):
        grp = slice(gi * SUBLANES, (gi + 1) * SUBLANES)
        for s in range(n_slabs):
            pwr, pwi = pws[s][:, :sp], pws[s][:, sp:]
            carry = carries[s]
            x = sloc_ref[s, grp, :]
            for sh in (1, 2, 4):
                sft = jnp.where(row_id >= sh, pltpu.roll(x, sh, axis=0), 0.0)
                tr, ti = _cmul(sft[:, :sp], sft[:, sp:], pwr[sh - 1:sh, :], pwi[sh - 1:sh, :])
                x = x + jnp.concatenate([tr, ti], axis=1)
            cr, ci = _cmul(carry[:, :sp], carry[:, sp:], pwr, pwi)
            incl = x + jnp.concatenate([cr, ci], axis=1)
            sprev_ref[s, grp, :] = jnp.where(row_id == 0, carry, pltpu.roll(incl, 1, axis=0))
            carries[s] = incl[SUBLANES - 1:SUBLANES, :]
    for s in range(n_slabs):
        carry_ref[s] = carries[s]

    for s in range(n_slabs):
        u_slab = slab_input(s)
        y_inter = _dot(sprev_ref[s].astype(BF16), wout_ref[s])
        for nt in range(lw // MXU_TILE):
            hi = (nt + 1) * MXU_TILE
            cols = slice(nt * MXU_TILE, hi)
            y = y_inter[:, cols] + _dot(u_slab[:, 0:hi], m_ref[s, 0:hi, cols])
            for i in range(nt * MXU_TILE // sw, hi // sw):
                lo = i * sw - nt * MXU_TILE
                y_ref[:, i * width + s * sw: i * width + (s + 1) * sw] = y[:, lo:lo + sw]

    for i in range(L):
        cols = slice(i * width, (i + 1) * width)
        yi = y_ref[:, cols] + d_ref[...] * u_ref[:, cols].astype(F32)
        hh = _gelu_tanh(yi)
        gate = _sigmoid(_dot(hh.astype(BF16), wglu_ref[...]) + bglu_ref[...])
        val = hh * gate
        for c in range(width // LANES):
            out_ref[c, pl.ds(i, rb, stride=L), :] = val[:, c * LANES:(c + 1) * LANES]
    for c in range(width // LANES):
        o_ref[:, c * LANES:(c + 1) * LANES] = out_ref[c].astype(BF16)


def _s5(u3, m, wst, wout, pw, d_skip, wglu, bglu, *, rb):
    b, rows, lw = u3.shape
    ns = m.shape[0]
    width = lw // S5_STEPS
    sp2 = wst.shape[2]
    kern = functools.partial(_s5_kernel, n_slabs=ns, width=width)
    return pl.pallas_call(
        kern,
        grid=(b, rows // rb),
        in_specs=[
            pl.BlockSpec((None, rb, lw), lambda i, j: (i, j, 0)),
            _resident(m.shape), _resident(wst.shape), _resident(wout.shape), _resident(pw.shape),
            _resident(d_skip.shape), _resident(wglu.shape), _resident(bglu.shape),
        ],
        out_specs=pl.BlockSpec((None, rb * S5_STEPS, width), lambda i, j: (i, j, 0)),
        out_shape=jax.ShapeDtypeStruct((b, rows * S5_STEPS, width), BF16),
        scratch_shapes=[
            pltpu.VMEM((ns, 1, sp2), F32),
            pltpu.VMEM((ns, rb, sp2), F32),
            pltpu.VMEM((ns, rb, sp2), F32),
            pltpu.VMEM((rb, lw), F32),
            pltpu.VMEM((width // LANES, rb * S5_STEPS, LANES), F32),
        ],
        compiler_params=_cparams(("parallel", "arbitrary")),
        name="s5",
    )(u3, m, wst, wout, pw, d_skip, wglu, bglu)


def _mix_kernel(x_ref, ga_ref, gb_ref, r_ref, gla_ref, ob_ref, wa_ref, wb_ref, wo_ref, o_ref):
    r = r_ref[...].astype(F32)
    o_a = (r * _sigmoid(r) * gla_ref[...].astype(F32)).astype(BF16)
    a = _dot(o_a, wa_ref[...])
    b = _dot(ob_ref[...], wb_ref[...])
    mix = _sigmoid(ga_ref[...].astype(F32)) * a + _sigmoid(gb_ref[...].astype(F32)) * b
    o_ref[...] = x_ref[...] + _dot(mix.astype(BF16), wo_ref[...])


def _mix(x2, main, gla, o_b, w_a, w_b, w_o, *, tm, col_r):
    n, d = x2.shape
    hv = gla.shape[1]
    return pl.pallas_call(
        _mix_kernel,
        grid=(n // tm,),
        in_specs=[
            pl.BlockSpec((tm, d), lambda i: (i, 0)),
            pl.BlockSpec((tm, d), lambda i: (i, 0)),
            pl.BlockSpec((tm, d), lambda i: (i, 1)),
            pl.BlockSpec((tm, hv), lambda i: (i, col_r // hv)),
            pl.BlockSpec((tm, hv), lambda i: (i, 0)),
            pl.BlockSpec((tm, o_b.shape[1]), lambda i: (i, 0)),
            _resident(w_a.shape), _resident(w_b.shape), _resident(w_o.shape),
        ],
        out_specs=pl.BlockSpec((tm, d), lambda i: (i, 0)),
        out_shape=jax.ShapeDtypeStruct((n, d), F32),
        compiler_params=_cparams(("parallel",)),
        name="mix",
    )(x2, main, main, main, gla, o_b, w_a, w_b, w_o)


def _ffn_kernel(x_ref, g2_ref, wg_ref, wu_ref, wd_ref, gf_ref, o_ref, h_ref):
    j = pl.program_id(1)

    @pl.when(j == 0)
    def _():
        x = x_ref[...]
        ms = jnp.mean(x * x, axis=-1, keepdims=True)
        h_ref[...] = (x * lax.rsqrt(ms + NORM_EPS) * g2_ref[...]).astype(BF16)
        o_ref[...] = x

    h = h_ref[...]
    gate = _dot(h, wg_ref[...])
    up = _dot(h, wu_ref[...])
    act = (gate * _sigmoid(gate) * up).astype(BF16)
    o_ref[...] += _dot(act, wd_ref[...])

    @pl.when(j == pl.num_programs(1) - 1)
    def _():
        y = o_ref[...]
        ms = jnp.mean(y * y, axis=-1, keepdims=True)
        o_ref[...] = y * lax.rsqrt(ms + NORM_EPS) * gf_ref[...]


def _ffn(x1, g2, w_in, w_out, gf, *, tm, tf):
    n, d = x1.shape
    f = w_out.shape[0]
    nf = f // tf
    return pl.pallas_call(
        _ffn_kernel,
        grid=(n // tm, nf),
        in_specs=[
            pl.BlockSpec((tm, d), lambda i, j: (i, 0)),
            pl.BlockSpec((1, d), lambda i, j: (0, 0)),
            pl.BlockSpec((d, tf), lambda i, j: (0, j)),
            pl.BlockSpec((d, tf), lambda i, j: (0, j + nf)),
            pl.BlockSpec((tf, d), lambda i, j: (j, 0)),
            pl.BlockSpec((1, d), lambda i, j: (0, 0)),
        ],
        out_specs=pl.BlockSpec((tm, d), lambda i, j: (i, 0)),
        out_shape=jax.ShapeDtypeStruct((n, d), F32),
        scratch_shapes=[pltpu.VMEM((tm, d), BF16)],
        compiler_params=_cparams(("parallel", "arbitrary")),
        name="ffn",
    )(x1, g2, w_in, w_in, w_out, gf)


def _tile(n, pref):
    t = min(n, pref)
    assert n % t == 0, (n, t)
    return t


def _layer(x2, bsz, seq, norm1_g, w_in, w_a2, b_a2, gla_norm_g, lam_re, lam_im, log_dt,
           s5_b_re, s5_b_im, s5_c_re, s5_c_im, s5_d, w_glu, b_glu,
           w_branch_a, w_branch_b, w_out, norm2_g, w_ffn_in, w_ffn_out, final_g):
    n, d = x2.shape
    hk = w_a2.shape[1]
    hv = gla_norm_g.shape[0]
    rank = w_a2.shape[0]
    sw = w_glu.shape[0]
    dk, dv = hk // GLA_HEADS, hv // GLA_HEADS

    o_q, o_k, o_v, o_r = 0, hk, 2 * hk, 2 * hk + hv
    o_al = 2 * hk + 2 * hv
    o_u = o_al + rank
    o_ga = o_u + sw
    o_gb = o_ga + d
    assert o_gb + d == w_in.shape[1]
    tr = sw
    segments = ((o_ga, 2 * d), (o_v, 2 * hv), (o_q, 2 * hk), (o_u, sw))
    starts = [lo + k for lo, width in segments for k in range(0, width, tr)]
    wt, wt_low = _w_in_prep(jnp.swapaxes(w_in, 0, 1), starts=starts, low_lo=o_al, rank=rank, tr=tr)
    wa2p = jnp.pad(w_a2, ((0, LANES - rank), (0, 0)))
    col_v, col_r, col_q, col_k = 2 * d, 2 * d + hv, 2 * d + 2 * hv, 2 * d + 2 * hv + hk

    main, u, loga = _in_proj(x2, norm1_g.reshape(1, d), wt, wt_low, wa2p, b_a2.reshape(1, hk),
                             tm=_tile(n, 1024), tn=1024, sw=sw)

    ts = _tile(seq, 512)
    tc = _tile(ts, 256)
    blk = jnp.arange(tc) // GLA_CHUNK
    tri = ((blk[:, None] == blk[None, :]) & (jnp.arange(tc)[:, None] >= jnp.arange(tc)[None, :]))
    gla = _gla(main.reshape(bsz, seq, -1), loga.reshape(bsz, seq, hk), gla_norm_g.reshape(1, hv),
               tri.astype(BF16), ts=ts, tc=tc, col_q=col_q, col_k=col_k, col_v=col_v, dk=dk, dv=dv)

    m, wst, wout, pw = _s5_tables(lam_re, lam_im, log_dt, s5_b_re, s5_b_im, s5_c_re, s5_c_im)
    rows = seq // S5_STEPS
    o_b = _s5(u.reshape(bsz, rows, S5_STEPS * sw), m, wst, wout, pw, s5_d.reshape(1, sw),
              w_glu.astype(BF16), b_glu.reshape(1, sw), rb=_tile(rows, 256))

    x1 = _mix(x2, main, gla.reshape(n, hv), o_b.reshape(n, sw), w_branch_a.astype(BF16),
              w_branch_b.astype(BF16), w_out.astype(BF16), tm=_tile(n, 512), col_r=col_r)
    return _ffn(x1, norm2_g.reshape(1, d), w_ffn_in.astype(BF16), w_ffn_out.astype(BF16),
                final_g.reshape(1, d), tm=_tile(n, 1024), tf=512)


def kernel(x, norm1_g, w_in, w_a2, b_a2, gla_norm_g, lam_re, lam_im, log_dt, s5_b_re, s5_b_im,
           s5_c_re, s5_c_im, s5_d, w_glu, b_glu, w_branch_a, w_branch_b, w_out, norm2_g,
           w_ffn_in, w_ffn_out, final_norm_g):
    bsz, seq, d = x.shape
    assert norm1_g.shape[0] == 1, "single-layer block"
    out = _layer(x.reshape(bsz * seq, d), bsz, seq, norm1_g[0], w_in[0], w_a2[0], b_a2[0],
                 gla_norm_g[0], lam_re[0], lam_im[0], log_dt[0], s5_b_re[0], s5_b_im[0],
                 s5_c_re[0], s5_c_im[0], s5_d[0], w_glu[0], b_glu[0], w_branch_a[0],
                 w_branch_b[0], w_out[0], norm2_g[0], w_ffn_in[0], w_ffn_out[0], final_norm_g[0])
    return out.reshape(bsz, seq, d)
```

```python
import functools
import math

import jax
import jax.numpy as jnp
from jax import lax
from jax.experimental import pallas as pl
from jax.experimental.pallas import tpu as pltpu

F32 = jnp.float32
BF16 = jnp.bfloat16

NORM_EPS = 1e-6
GLA_HEADS = 4
GLA_CHUNK = 32
GLA_GATE_TAU = 16.0
LOG2_E = math.log2(math.e)
LANES = 128
SUBLANES = 8
MXU_TILE = 256
ROW_UNIT = 16
S5_STEPS = 8
S5_SLAB_GROUPS = 8
VMEM_LIMIT = 56 * 1024 * 1024


def _sigmoid(x):
    return 1.0 / (1.0 + jnp.exp(-x))


def _log_sigmoid(x):
    return jnp.minimum(x, 0.0) - jnp.log(1.0 + jnp.exp(-jnp.abs(x)))


def _gelu_tanh(x):
    c = math.sqrt(2.0 / math.pi)
    return 0.5 * x * (1.0 + jnp.tanh(c * (x + 0.044715 * (x * x * x))))


def _dot(a, b):
    return jnp.dot(a, b, preferred_element_type=F32)


def _cparams(sem):
    return pltpu.CompilerParams(dimension_semantics=sem, vmem_limit_bytes=VMEM_LIMIT)


def _resident(shape):
    zeros = (0,) * len(shape)
    return pl.BlockSpec(shape, lambda *_: zeros, pipeline_mode=pl.Buffered(1))


def _w_in_prep_kernel(starts_ref, w_ref, wlow_ref, out_ref, low_ref, *, rank):
    del starts_ref
    out_ref[...] = jnp.transpose(w_ref[...]).astype(BF16)

    @pl.when(pl.program_id(0) == 0)
    def _():
        lane = lax.broadcasted_iota(jnp.int32, low_ref.shape, 1)
        low_ref[...] = jnp.where(lane < rank, jnp.transpose(wlow_ref[...]), 0.0).astype(BF16)


def _w_in_prep(wt, *, starts, low_lo, rank, tr):
    _, d = wt.shape
    assert all(s % ROW_UNIT == 0 for s in starts) and low_lo % ROW_UNIT == 0
    n_blocks = len(starts)
    grid_spec = pltpu.PrefetchScalarGridSpec(
        num_scalar_prefetch=1,
        grid=(n_blocks,),
        in_specs=[
            pl.BlockSpec((pl.Element(tr), pl.Element(d)), lambda i, st: (st[i] * ROW_UNIT, 0)),
            pl.BlockSpec((pl.Element(LANES), pl.Element(d)), lambda i, st: (low_lo, 0)),
        ],
        out_specs=[
            pl.BlockSpec((d, tr), lambda i, st: (0, i)),
            pl.BlockSpec((d, LANES), lambda i, st: (0, 0)),
        ],
    )
    return pl.pallas_call(
        functools.partial(_w_in_prep_kernel, rank=rank),
        grid_spec=grid_spec,
        out_shape=[
            jax.ShapeDtypeStruct((d, n_blocks * tr), BF16),
            jax.ShapeDtypeStruct((d, LANES), BF16),
        ],
        compiler_params=_cparams(("arbitrary",)),
        name="w_in_prep",
    )(jnp.asarray([s // ROW_UNIT for s in starts], jnp.int32), wt, wt)


def _in_proj_kernel(x_ref, g_ref, w_ref, wu_ref, wlow_ref, wa2_ref, ba2_ref,
                    main_ref, u_ref, loga_ref, h0_ref, h1_ref, uacc_ref, *, n_main):
    i = pl.program_id(0)
    j = pl.program_id(1)
    chunk = x_ref.shape[0]

    def normalise_chunk(h_ref):
        x = x_ref[...]
        ms = jnp.mean(x * x, axis=-1, keepdims=True)
        rows = pl.ds(pl.multiple_of(j * chunk, chunk), chunk)
        h_ref[rows, :] = (x * lax.rsqrt(ms + NORM_EPS) * g_ref[...]).astype(BF16)

    def tail(h):
        a_low = _dot(h, wlow_ref[...])
        z = _dot(a_low, wa2_ref[...]) + ba2_ref[...]
        loga_ref[...] = _log_sigmoid(z) * (LOG2_E / GLA_GATE_TAU)
        uacc = _dot(h, wu_ref[...])
        rows, width = u_ref.shape[0], wu_ref.shape[1]
        for c in range(width // LANES):
            uacc_ref[c] = uacc[:, c * LANES:(c + 1) * LANES]
        for s in range(S5_STEPS):
            for c in range(width // LANES):
                lo = s * width + c * LANES
                u_ref[:, lo:lo + LANES] = uacc_ref[c, pl.ds(s, rows, stride=S5_STEPS), :].astype(BF16)

    @pl.when(i == 0)
    def _():
        normalise_chunk(h0_ref)

    for parity, (h_write, h_read) in enumerate(((h0_ref, h1_ref), (h1_ref, h0_ref))):
        active = (i > 0) & (i % 2 == parity)

        @pl.when(active & (j < n_main))
        def _(h_write=h_write, h_read=h_read):
            normalise_chunk(h_write)
            main_ref[...] = _dot(h_read[...], w_ref[...]).astype(BF16)

        @pl.when(active & (j == n_main))
        def _(h_write=h_write, h_read=h_read):
            normalise_chunk(h_write)
            tail(h_read[...])


def _in_proj(x2, g, w, w_low, wa2p, ba2, *, tm, tn, sw):
    n, d = x2.shape
    n_main = (w.shape[1] - sw) // tn
    hk = wa2p.shape[1]
    kern = functools.partial(_in_proj_kernel, n_main=n_main)
    nt = n // tm
    n_steps = n_main + 1
    assert tm % n_steps == 0
    chunk = tm // n_steps

    def prev(i):
        return jnp.maximum(i - 1, 0)

    return pl.pallas_call(
        kern,
        grid=(nt + 1, n_steps),
        in_specs=[
            pl.BlockSpec((chunk, d), lambda i, j: (jnp.minimum(i, nt - 1) * n_steps + j, 0)),
            pl.BlockSpec((1, d), lambda i, j: (0, 0)),
            pl.BlockSpec((d, tn), lambda i, j: (0, jnp.minimum(j, n_main - 1))),
            pl.BlockSpec((d, sw), lambda i, j: (0, n_main * tn // sw), pipeline_mode=pl.Buffered(1)),
            _resident(w_low.shape),
            _resident(wa2p.shape),
            _resident(ba2.shape),
        ],
        out_specs=[
            pl.BlockSpec((tm, tn), lambda i, j: (prev(i), jnp.where(i == 0, 0, jnp.minimum(j, n_main - 1)))),
            pl.BlockSpec((tm // S5_STEPS, S5_STEPS * sw), lambda i, j: (prev(i), 0)),
            pl.BlockSpec((tm, hk), lambda i, j: (prev(i), 0)),
        ],
        out_shape=[
            jax.ShapeDtypeStruct((n, n_main * tn), BF16),
            jax.ShapeDtypeStruct((n // S5_STEPS, S5_STEPS * sw), BF16),
            jax.ShapeDtypeStruct((n, hk), F32),
        ],
        scratch_shapes=[pltpu.VMEM((tm, d), BF16), pltpu.VMEM((tm, d), BF16),
                        pltpu.VMEM((sw // LANES, tm, LANES), F32)],
        compiler_params=_cparams(("arbitrary", "arbitrary")),
        name="in_proj",
    )(x2, g, w, w, w_low, wa2p, ba2)


def _kr_base(m):
    return GLA_CHUNK * (m * (m + 1) // 2)


def _gla_block(q_ref, k_ref, v_ref, loga_ref, ghn_ref, tri_ref,
               o_ref, s_ref, qd_ref, qb_ref, kb_ref, kr_ref, p_ref, *, n_sub, dk, dv):
    c = GLA_CHUNK
    shared = {}

    def decays():
        log_a = loga_ref[...]
        la_hi = log_a.astype(BF16)
        la_lo = (log_a - la_hi.astype(F32)).astype(BF16)
        tri = tri_ref[...]
        bc_all = _dot(tri, la_hi) + _dot(tri, la_lo)
        bl = [bc_all[(i + 1) * c - 1:(i + 1) * c, :] for i in range(n_sub)]
        bs = [jnp.zeros_like(bl[0])]
        for i in range(n_sub):
            bs.append(bs[-1] + bl[i])
        shared.update(bc_all=bc_all, bl=bl, bs=bs)

    def operands(i):
        bc_all, bl, bs = shared["bc_all"], shared["bl"], shared["bs"]
        btot = bs[n_sub]
        rows = slice(i * c, (i + 1) * c)
        b = bc_all[rows, :]
        q = q_ref[rows, :].astype(F32) * (dk ** -0.5)
        k = k_ref[rows, :].astype(F32)
        qd = q * jnp.exp2(b)
        ks = k * jnp.exp2(bl[i] - b)
        qd_ref[rows, :] = qd.astype(BF16)
        qb_ref[rows, :] = (qd * jnp.exp2(bs[i])).astype(BF16)
        kb_ref[rows, :] = (ks * jnp.exp2(btot - bs[i + 1])).astype(BF16)
        own = _kr_base(i) + i * c
        kr_ref[own:own + c, :] = (k * jnp.exp2(-b)).astype(BF16)
        for m in range(i + 1, n_sub):
            lo = _kr_base(m) + i * c
            kr_ref[lo:lo + c, :] = (ks * jnp.exp2(bs[m] - bs[i + 1])).astype(BF16)

    def scores(h):
        row = lax.broadcasted_iota(jnp.int32, (c, c), 0)
        col = lax.broadcasted_iota(jnp.int32, (c, c), 1)
        causal = row >= col
        ks_ = slice(h * dk, (h + 1) * dk)
        for m in range(n_sub):
            rows = slice(m * c, (m + 1) * c)
            keys = slice(_kr_base(m), _kr_base(m) + (m + 1) * c)
            s = lax.dot_general(qd_ref[rows, ks_], kr_ref[keys, ks_], (((1,), (1,)), ((), ())),
                                preferred_element_type=F32)
            if m > 0:
                p_ref[h, rows, 0:m * c] = s[:, 0:m * c].astype(BF16)
            p_ref[h, rows, m * c:(m + 1) * c] = jnp.where(causal, s[:, m * c:], 0.0).astype(BF16)

    def head(h):
        btot = shared["bs"][n_sub]
        ks_ = slice(h * dk, (h + 1) * dk)
        vs_ = slice(h * dv, (h + 1) * dv)
        v = v_ref[:, vs_]
        state = s_ref[h]
        o = _dot(p_ref[h], v) + _dot(qb_ref[:, ks_], state.astype(BF16))

        upd = lax.dot_general(kb_ref[:, ks_], v, (((0,), (0,)), ((), ())),
                              preferred_element_type=F32)
        decay_col = jnp.transpose(jnp.broadcast_to(jnp.exp2(btot[:, ks_]), (dk, dk)))
        if dv > dk:
            decay_col = jnp.concatenate([decay_col] * (dv // dk), axis=1)
        s_ref[h] = decay_col * state + upd
        shared["o", h] = o

    def finish(h):
        vs_ = slice(h * dv, (h + 1) * dv)
        o = shared["o", h]
        o = o * lax.rsqrt(jnp.mean(o * o, axis=-1, keepdims=True) + NORM_EPS)
        o_ref[:, vs_] = (o * ghn_ref[:, vs_]).astype(BF16)

    stages = [functools.partial(scores, 0), functools.partial(scores, 1)]
    for h in range(GLA_HEADS):
        if h + 2 < GLA_HEADS:
            stages.append(functools.partial(scores, h + 2))
        stages.append(functools.partial(head, h))
        if h > 0:
            stages.append(functools.partial(finish, h - 1))
    stages.append(functools.partial(finish, GLA_HEADS - 1))
    return decays, [functools.partial(operands, i) for i in range(n_sub)], stages


def _gla_kernel(q_ref, k_ref, v_ref, loga_ref, ghn_ref, tri_ref,
                o_ref, s_ref, qd_ref, qb_ref, kb_ref, kr_ref, p_ref, *, tc, dk, dv):
    @pl.when(pl.program_id(1) == 0)
    def _():
        s_ref[...] = jnp.zeros_like(s_ref)
        p_ref[...] = jnp.zeros_like(p_ref)

    blocks = []
    for blk in range(q_ref.shape[0] // tc):
        rows = pl.ds(blk * tc, tc)
        blocks.append(_gla_block(
            q_ref.at[rows], k_ref.at[rows], v_ref.at[rows], loga_ref.at[rows], ghn_ref, tri_ref,
            o_ref.at[rows], s_ref, qd_ref.at[blk], qb_ref.at[blk], kb_ref.at[blk], kr_ref.at[blk],
            p_ref.at[blk], n_sub=tc // GLA_CHUNK, dk=dk, dv=dv))

    decays, operands, _ = blocks[0]
    decays()
    for stage in operands:
        stage()
    for blk, (_, _, heads) in enumerate(blocks):
        nxt = blocks[blk + 1] if blk + 1 < len(blocks) else None
        if nxt is not None:
            nxt[0]()
        per_stage = -(-len(nxt[1]) // len(heads)) if nxt is not None else 0
        for h, stage in enumerate(heads):
            stage()
            if nxt is not None:
                for operand_stage in nxt[1][h * per_stage:(h + 1) * per_stage]:
                    operand_stage()


def _gla(main3, loga3, ghn, tri, *, ts, tc, col_q, col_k, col_v, dk, dv):
    b, t, _ = main3.shape
    hk, hv = GLA_HEADS * dk, GLA_HEADS * dv
    kern = functools.partial(_gla_kernel, tc=tc, dk=dk, dv=dv)
    nb = ts // tc
    return pl.pallas_call(
        kern,
        grid=(b, t // ts),
        in_specs=[
            pl.BlockSpec((None, ts, hk), lambda i, j: (i, j, col_q // hk)),
            pl.BlockSpec((None, ts, hk), lambda i, j: (i, j, col_k // hk)),
            pl.BlockSpec((None, ts, hv), lambda i, j: (i, j, col_v // hv)),
            pl.BlockSpec((None, ts, hk), lambda i, j: (i, j, 0)),
            pl.BlockSpec((1, hv), lambda i, j: (0, 0)),
            pl.BlockSpec((tc, tc), lambda i, j: (0, 0)),
        ],
        out_specs=pl.BlockSpec((None, ts, hv), lambda i, j: (i, j, 0)),
        out_shape=jax.ShapeDtypeStruct((b, t, hv), BF16),
        scratch_shapes=[
            pltpu.VMEM((GLA_HEADS, dk, dv), F32),
            pltpu.VMEM((nb, tc, hk), BF16),
            pltpu.VMEM((nb, tc, hk), BF16),
            pltpu.VMEM((nb, tc, hk), BF16),
            pltpu.VMEM((nb, _kr_base(tc // GLA_CHUNK), hk), BF16),
            pltpu.VMEM((nb, GLA_HEADS, tc, tc), BF16),
        ],
        compiler_params=_cparams(("parallel", "arbitrary")),
        name="gla",
    )(main3, main3, main3, loga3, ghn, tri)


def _cmul(ar, ai, br, bi):
    return ar * br - ai * bi, ar * bi + ai * br


def _s5_tables_kernel(lr_r, li_r, ldt_r, lr_c, li_c, ldt_c, bre, bim, cre, cim,
                      m_ref, wst_ref, wout_ref, pw_ref, *, gc, p):
    L = S5_STEPS
    sw = S5_SLAB_GROUPS * gc
    sp = S5_SLAB_GROUPS * p

    def discretise(lr, li, ldt):
        dt = jnp.exp(ldt)
        mag = jnp.exp(lr * dt)
        return mag * jnp.cos(li * dt), mag * jnp.sin(li * dt)

    lr, li = lr_r[...], li_r[...]
    ar, ai = discretise(lr, li, ldt_r[...])
    den = lr * lr + li * li
    am1 = ar - 1.0
    f_re = (am1 * lr + ai * li) / den
    f_im = (ai * lr - am1 * li) / den

    rg = lax.broadcasted_iota(jnp.int32, (sw, sp), 0) // gc
    lg = lax.broadcasted_iota(jnp.int32, (sw, sp), 1) // p
    bmask = rg == lg
    br, bi = bre[...], bim[...]
    bbr = jnp.where(bmask, f_re * br - f_im * bi, 0.0)
    bbi = jnp.where(bmask, f_re * bi + f_im * br, 0.0)

    rg2 = lax.broadcasted_iota(jnp.int32, (sp, sw), 0) // p
    lg2 = lax.broadcasted_iota(jnp.int32, (sp, sw), 1) // gc
    cmask = rg2 == lg2
    cr = jnp.where(cmask, cre[...], 0.0)
    ci = jnp.where(cmask, cim[...], 0.0)

    pr, pi = [jnp.ones_like(ar)], [jnp.zeros_like(ai)]
    for _ in range(L):
        nr, ni = _cmul(pr[-1], pi[-1], ar, ai)
        pr.append(nr)
        pi.append(ni)

    for j in range(L):
        xr, xi = _cmul(bbr, bbi, pr[L - 1 - j], pi[L - 1 - j])
        wst_ref[j * sw:(j + 1) * sw, 0:sp] = xr.astype(BF16)
        wst_ref[j * sw:(j + 1) * sw, sp:2 * sp] = xi.astype(BF16)

    kern = []
    for k in range(L):
        xr, xi = _cmul(bbr, bbi, pr[k], pi[k])
        kk = (jnp.dot(xr, cr, preferred_element_type=F32, precision=lax.Precision.HIGHEST)
              - jnp.dot(xi, ci, preferred_element_type=F32, precision=lax.Precision.HIGHEST))
        kern.append(kk.astype(BF16))
    zero = jnp.zeros((sw, sw), BF16)
    for j in range(L):
        for i in range(L):
            m_ref[j * sw:(j + 1) * sw, i * sw:(i + 1) * sw] = kern[i - j] if i >= j else zero

    acr, aci = discretise(lr_c[...], li_c[...], ldt_c[...])
    qr, qi = acr, aci
    for i in range(L):
        zr, zi = _cmul(cr, ci, qr, qi)
        wout_ref[0:sp, i * sw:(i + 1) * sw] = zr.astype(BF16)
        wout_ref[sp:2 * sp, i * sw:(i + 1) * sw] = (-zi).astype(BF16)
        qr, qi = _cmul(qr, qi, acr, aci)

    a8r, a8i = pr[L], pi[L]
    tr, ti = a8r, a8i
    for i in range(SUBLANES):
        pw_ref[i:i + 1, 0:sp] = tr
        pw_ref[i:i + 1, sp:2 * sp] = ti
        tr, ti = _cmul(tr, ti, a8r, a8i)


def _s5_tables(lam_re, lam_im, log_dt, b_re, b_im, c_re, c_im):
    g, p = lam_re.shape
    gc = b_re.shape[2]
    ns = g // S5_SLAB_GROUPS
    sw, sp = S5_SLAB_GROUPS * gc, S5_SLAB_GROUPS * p
    lw = S5_STEPS * sw

    def rows(a):
        return a.reshape(ns, 1, sp)

    def cols(a):
        return a.reshape(ns, sp, 1)

    ldt = jnp.broadcast_to(log_dt[:, None], (g, p))

    def b_layout(b):
        bt = jnp.transpose(b, (0, 2, 1)).reshape(ns, sw, p)
        return jnp.tile(bt, (1, 1, S5_SLAB_GROUPS))

    def c_layout(c):
        ct = jnp.transpose(c.reshape(ns, S5_SLAB_GROUPS, gc, p), (0, 3, 1, 2)).reshape(ns, p, sw)
        return jnp.tile(ct, (1, S5_SLAB_GROUPS, 1))

    row_spec = pl.BlockSpec((None, 1, sp), lambda s: (s, 0, 0))
    col_spec = pl.BlockSpec((None, sp, 1), lambda s: (s, 0, 0))
    b_spec = pl.BlockSpec((None, sw, sp), lambda s: (s, 0, 0))
    c_spec = pl.BlockSpec((None, sp, sw), lambda s: (s, 0, 0))
    kern = functools.partial(_s5_tables_kernel, gc=gc, p=p)
    return pl.pallas_call(
        kern,
        grid=(ns,),
        in_specs=[row_spec, row_spec, row_spec, col_spec, col_spec, col_spec,
                  b_spec, b_spec, c_spec, c_spec],
        out_specs=[
            pl.BlockSpec((None, lw, lw), lambda s: (s, 0, 0)),
            pl.BlockSpec((None, lw, 2 * sp), lambda s: (s, 0, 0)),
            pl.BlockSpec((None, 2 * sp, lw), lambda s: (s, 0, 0)),
            pl.BlockSpec((None, SUBLANES, 2 * sp), lambda s: (s, 0, 0)),
        ],
        out_shape=[
            jax.ShapeDtypeStruct((ns, lw, lw), BF16),
            jax.ShapeDtypeStruct((ns, lw, 2 * sp), BF16),
            jax.ShapeDtypeStruct((ns, 2 * sp, lw), BF16),
            jax.ShapeDtypeStruct((ns, SUBLANES, 2 * sp), F32),
        ],
        compiler_params=_cparams(("parallel",)),
        name="s5_tables",
    )(rows(lam_re), rows(lam_im), rows(ldt), cols(lam_re), cols(lam_im), cols(ldt),
      b_layout(b_re), b_layout(b_im), c_layout(c_re), c_layout(c_im))


def _s5_kernel(u_ref, m_ref, wst_ref, wout_ref, pw_ref, d_ref, wglu_ref, bglu_ref,
               o_ref, carry_ref, sloc_ref, sprev_ref, y_ref, out_ref, *, n_slabs, width):
    L = S5_STEPS
    sw = LANES
    lw = L * sw
    rb = u_ref.shape[0]
    sp2 = sloc_ref.shape[2]
    sp = sp2 // 2

    @pl.when(pl.program_id(1) == 0)
    def _():
        carry_ref[...] = jnp.zeros_like(carry_ref)

    row_id = lax.broadcasted_iota(jnp.int32, (SUBLANES, sp2), 0)

    def slab_input(s):
        return jnp.concatenate(
            [u_ref[:, j * width + s * sw: j * width + (s + 1) * sw] for j in range(L)], axis=1)

    for s in range(n_slabs):
        sloc_ref[s] = _dot(slab_input(s), wst_ref[s])

    pws = [pw_ref[s] for s in range(n_slabs)]
    steps = [{sh: jnp.where(row_id >= sh, pws[s][sh - 1:sh, :], 0.0) for sh in (1, 2, 4)}
             for s in range(n_slabs)]
    carries = [carry_ref[s] for s in range(n_slabs)]
    for gi in range(rb // SUBLANES):
        grp = slice(gi * SUBLANES, (gi + 1) * SUBLANES)
        for s in range(n_slabs):
            pwr, pwi = pws[s][:, :sp], pws[s][:, sp:]
            carry = carries[s]
            x = sloc_ref[s, grp, :]
            for sh in (1, 2, 4):
                sft = pltpu.roll(x, sh, axis=0)
                mult = steps[s][sh]
                tr, ti = _cmul(sft[:, :sp], sft[:, sp:], mult[:, :sp], mult[:, sp:])
                x = x + jnp.concatenate([tr, ti], axis=1)
            cr, ci = _cmul(carry[:, :sp], carry[:, sp:], pwr, pwi)
            incl = x + jnp.concatenate([cr, ci], axis=1)
            sprev_ref[s, grp, :] = jnp.where(row_id == 0, carry, pltpu.roll(incl, 1, axis=0))
            carries[s] = incl[SUBLANES - 1:SUBLANES, :]
    for s in range(n_slabs):
        carry_ref[s] = carries[s]

    for s in range(n_slabs):
        u_slab = slab_input(s)
        y_inter = _dot(sprev_ref[s].astype(BF16), wout_ref[s])
        for nt in range(lw // MXU_TILE):
            hi = (nt + 1) * MXU_TILE
            cols = slice(nt * MXU_TILE, hi)
            y = y_inter[:, cols] + _dot(u_slab[:, 0:hi], m_ref[s, 0:hi, cols])
            for i in range(nt * MXU_TILE // sw, hi // sw):
                lo = i * sw - nt * MXU_TILE
                y_ref[:, i * width + s * sw: i * width + (s + 1) * sw] = y[:, lo:lo + sw]

    for i in range(L):
        cols = slice(i * width, (i + 1) * width)
        yi = y_ref[:, cols] + d_ref[...] * u_ref[:, cols].astype(F32)
        hh = _gelu_tanh(yi)
        gate = _sigmoid(_dot(hh.astype(BF16), wglu_ref[...]) + bglu_ref[...])
        val = hh * gate
        for c in range(width // LANES):
            out_ref[c, pl.ds(i, rb, stride=L), :] = val[:, c * LANES:(c + 1) * LANES]
    for c in range(width // LANES):
        o_ref[:, c * LANES:(c + 1) * LANES] = out_ref[c].astype(BF16)


def _s5(u3, m, wst, wout, pw, d_skip, wglu, bglu, *, rb):
    b, rows, lw = u3.shape
    ns = m.shape[0]
    width = lw // S5_STEPS
    sp2 = wst.shape[2]
    kern = functools.partial(_s5_kernel, n_slabs=ns, width=width)
    return pl.pallas_call(
        kern,
        grid=(b, rows // rb),
        in_specs=[
            pl.BlockSpec((None, rb, lw), lambda i, j: (i, j, 0)),
            _resident(m.shape), _resident(wst.shape), _resident(wout.shape), _resident(pw.shape),
            _resident(d_skip.shape), _resident(wglu.shape), _resident(bglu.shape),
        ],
        out_specs=pl.BlockSpec((None, rb * S5_STEPS, width), lambda i, j: (i, j, 0)),
        out_shape=jax.ShapeDtypeStruct((b, rows * S5_STEPS, width), BF16),
        scratch_shapes=[
            pltpu.VMEM((ns, 1, sp2), F32),
            pltpu.VMEM((ns, rb, sp2), F32),
            pltpu.VMEM((ns, rb, sp2), F32),
            pltpu.VMEM((rb, lw), F32),
            pltpu.VMEM((width // LANES, rb * S5_STEPS, LANES), F32),
        ],
        compiler_params=_cparams(("parallel", "arbitrary")),
        name="s5",
    )(u3, m, wst, wout, pw, d_skip, wglu, bglu)


def _mix_kernel(x_ref, ga_ref, gb_ref, r_ref, gla_ref, ob_ref, wa_ref, wb_ref, wo_ref, o_ref):
    r = r_ref[...].astype(F32)
    o_a = (r * _sigmoid(r) * gla_ref[...].astype(F32)).astype(BF16)
    a = _dot(o_a, wa_ref[...])
    b = _dot(ob_ref[...], wb_ref[...])
    mix = _sigmoid(ga_ref[...].astype(F32)) * a + _sigmoid(gb_ref[...].astype(F32)) * b
    o_ref[...] = x_ref[...] + _dot(mix.astype(BF16), wo_ref[...])


def _mix(x2, main, gla, o_b, w_a, w_b, w_o, *, tm, col_r):
    n, d = x2.shape
    hv = gla.shape[1]
    return pl.pallas_call(
        _mix_kernel,
        grid=(n // tm,),
        in_specs=[
            pl.BlockSpec((tm, d), lambda i: (i, 0)),
            pl.BlockSpec((tm, d), lambda i: (i, 0)),
            pl.BlockSpec((tm, d), lambda i: (i, 1)),
            pl.BlockSpec((tm, hv), lambda i: (i, col_r // hv)),
            pl.BlockSpec((tm, hv), lambda i: (i, 0)),
            pl.BlockSpec((tm, o_b.shape[1]), lambda i: (i, 0)),
            _resident(w_a.shape), _resident(w_b.shape), _resident(w_o.shape),
        ],
        out_specs=pl.BlockSpec((tm, d), lambda i: (i, 0)),
        out_shape=jax.ShapeDtypeStruct((n, d), F32),
        compiler_params=_cparams(("parallel",)),
        name="mix",
    )(x2, main, main, main, gla, o_b, w_a, w_b, w_o)


def _ffn_kernel(x_ref, g2_ref, wg_ref, wu_ref, wd_ref, gf_ref, o_ref, h_ref):
    j = pl.program_id(1)

    @pl.when(j == 0)
    def _():
        x = x_ref[...]
        ms = jnp.mean(x * x, axis=-1, keepdims=True)
        h_ref[...] = (x * lax.rsqrt(ms + NORM_EPS) * g2_ref[...]).astype(BF16)
        o_ref[...] = x

    h = h_ref[...]
    gate = _dot(h, wg_ref[...])
    up = _dot(h, wu_ref[...])
    act = (gate * _sigmoid(gate) * up).astype(BF16)
    o_ref[...] += _dot(act, wd_ref[...])

    @pl.when(j == pl.num_programs(1) - 1)
    def _():
        y = o_ref[...]
        ms = jnp.mean(y * y, axis=-1, keepdims=True)
        o_ref[...] = y * lax.rsqrt(ms + NORM_EPS) * gf_ref[...]


def _ffn(x1, g2, w_in, w_out, gf, *, tm, tf):
    n, d = x1.shape
    f = w_out.shape[0]
    nf = f // tf
    return pl.pallas_call(
        _ffn_kernel,
        grid=(n // tm, nf),
        in_specs=[
            pl.BlockSpec((tm, d), lambda i, j: (i, 0)),
            pl.BlockSpec((1, d), lambda i, j: (0, 0)),
            pl.BlockSpec((d, tf), lambda i, j: (0, j)),
            pl.BlockSpec((d, tf), lambda i, j: (0, j + nf)),
            pl.BlockSpec((tf, d), lambda i, j: (j, 0)),
            pl.BlockSpec((1, d), lambda i, j: (0, 0)),
        ],
        out_specs=pl.BlockSpec((tm, d), lambda i, j: (i, 0)),
        out_shape=jax.ShapeDtypeStruct((n, d), F32),
        scratch_shapes=[pltpu.VMEM((tm, d), BF16)],
        compiler_params=_cparams(("parallel", "arbitrary")),
        name="ffn",
    )(x1, g2, w_in, w_in, w_out, gf)


def _tile(n, pref):
    t = min(n, pref)
    assert n % t == 0, (n, t)
    return t


def _layer(x2, bsz, seq, norm1_g, w_in, w_a2, b_a2, gla_norm_g, lam_re, lam_im, log_dt,
           s5_b_re, s5_b_im, s5_c_re, s5_c_im, s5_d, w_glu, b_glu,
           w_branch_a, w_branch_b, w_out, norm2_g, w_ffn_in, w_ffn_out, final_g):
    n, d = x2.shape
    hk = w_a2.shape[1]
    hv = gla_norm_g.shape[0]
    rank = w_a2.shape[0]
    sw = w_glu.shape[0]
    dk, dv = hk // GLA_HEADS, hv // GLA_HEADS

    o_q, o_k, o_v, o_r = 0, hk, 2 * hk, 2 * hk + hv
    o_al = 2 * hk + 2 * hv
    o_u = o_al + rank
    o_ga = o_u + sw
    o_gb = o_ga + d
    assert o_gb + d == w_in.shape[1]
    tr = sw
    segments = ((o_ga, 2 * d), (o_v, 2 * hv), (o_q, 2 * hk), (o_u, sw))
    starts = [lo + k for lo, width in segments for k in range(0, width, tr)]
    w_re, w_low = _w_in_prep(jnp.swapaxes(w_in, 0, 1), starts=starts, low_lo=o_al, rank=rank, tr=tr)
    wa2p = jnp.pad(w_a2, ((0, LANES - rank), (0, 0)))
    col_v, col_r, col_q, col_k = 2 * d, 2 * d + hv, 2 * d + 2 * hv, 2 * d + 2 * hv + hk

    main, u, loga = _in_proj(x2, norm1_g.reshape(1, d), w_re, w_low, wa2p, b_a2.reshape(1, hk),
                             tm=_tile(n, 1024), tn=1024, sw=sw)

    ts = _tile(seq, 512)
    tc = _tile(ts, 256)
    blk = jnp.arange(tc) // GLA_CHUNK
    tri = ((blk[:, None] == blk[None, :]) & (jnp.arange(tc)[:, None] >= jnp.arange(tc)[None, :]))
    gla = _gla(main.reshape(bsz, seq, -1), loga.reshape(bsz, seq, hk), gla_norm_g.reshape(1, hv),
               tri.astype(BF16), ts=ts, tc=tc, col_q=col_q, col_k=col_k, col_v=col_v, dk=dk, dv=dv)

    m, wst, wout, pw = _s5_tables(lam_re, lam_im, log_dt, s5_b_re, s5_b_im, s5_c_re, s5_c_im)
    rows = seq // S5_STEPS
    o_b = _s5(u.reshape(bsz, rows, S5_STEPS * sw), m, wst, wout, pw, s5_d.reshape(1, sw),
              w_glu.astype(BF16), b_glu.reshape(1, sw), rb=_tile(rows, 256))

    x1 = _mix(x2, main, gla.reshape(n, hv), o_b.reshape(n, sw), w_branch_a.astype(BF16),
              w_branch_b.astype(BF16), w_out.astype(BF16), tm=_tile(n, 512), col_r=col_r)
    return _ffn(x1, norm2_g.reshape(1, d), w_ffn_in.astype(BF16), w_ffn_out.astype(BF16),
                final_g.reshape(1, d), tm=_tile(n, 1024), tf=512)


def kernel(x, norm1_g, w_in, w_a2, b_a2, gla_norm_g, lam_re, lam_im, log_dt, s5_b_re, s5_b_im,
           s5_c_re, s5_c_im, s5_d, w_glu, b_glu, w_branch_a, w_branch_b, w_out, norm2_g,
           w_ffn_in, w_ffn_out, final_norm_g):
    bsz, seq, d = x.shape
    assert norm1_g.shape[0] == 1, "single-layer block"
    out = _layer(x.reshape(bsz * seq, d), bsz, seq, norm1_g[0], w_in[0], w_a2[0], b_a2[0],
                 gla_norm_g[0], lam_re[0], lam_im[0], log_dt[0], s5_b_re[0], s5_b_im[0],
                 s5_c_re[0], s5_c_im[0], s5_d[0], w_glu[0], b_glu[0], w_branch_a[0],
                 w_branch_b[0], w_out[0], norm2_g[0], w_ffn_in[0], w_ffn_out[0], final_norm_g[0])
    return out.reshape(bsz, seq, d)
```

```python
import functools
import math

import jax
import jax.numpy as jnp
from jax import lax
from jax.experimental import pallas as pl
from jax.experimental.pallas import tpu as pltpu

F32 = jnp.float32
BF16 = jnp.bfloat16

NORM_EPS = 1e-6
GLA_HEADS = 4
GLA_CHUNK = 32
GLA_GATE_TAU = 16.0
LOG2_E = math.log2(math.e)
LANES = 128
SUBLANES = 8
MXU_TILE = 256
ROW_UNIT = 16
S5_STEPS = 8
S5_SLAB_GROUPS = 8
VMEM_LIMIT = 56 * 1024 * 1024


def _sigmoid(x):
    return 1.0 / (1.0 + jnp.exp(-x))


def _log_sigmoid(x):
    return jnp.minimum(x, 0.0) - jnp.log(1.0 + jnp.exp(-jnp.abs(x)))


def _gelu_tanh(x):
    c = math.sqrt(2.0 / math.pi)
    return 0.5 * x * (1.0 + jnp.tanh(c * (x + 0.044715 * (x * x * x))))


def _dot(a, b):
    return jnp.dot(a, b, preferred_element_type=F32)


def _cparams(sem):
    return pltpu.CompilerParams(dimension_semantics=sem, vmem_limit_bytes=VMEM_LIMIT)


def _resident(shape):
    zeros = (0,) * len(shape)
    return pl.BlockSpec(shape, lambda *_: zeros, pipeline_mode=pl.Buffered(1))


def _w_in_prep_kernel(starts_ref, w_ref, wlow_ref, out_ref, low_ref, *, rank):
    del starts_ref
    out_ref[...] = jnp.transpose(w_ref[...]).astype(BF16)

    @pl.when(pl.program_id(0) == 0)
    def _():
        lane = lax.broadcasted_iota(jnp.int32, low_ref.shape, 1)
        low_ref[...] = jnp.where(lane < rank, jnp.transpose(wlow_ref[...]), 0.0).astype(BF16)


def _w_in_prep(wt, *, starts, low_lo, rank, tr):
    _, d = wt.shape
    assert all(s % ROW_UNIT == 0 for s in starts) and low_lo % ROW_UNIT == 0
    n_blocks = len(starts)
    grid_spec = pltpu.PrefetchScalarGridSpec(
        num_scalar_prefetch=1,
        grid=(n_blocks,),
        in_specs=[
            pl.BlockSpec((pl.Element(tr), pl.Element(d)), lambda i, st: (st[i] * ROW_UNIT, 0)),
            pl.BlockSpec((pl.Element(LANES), pl.Element(d)), lambda i, st: (low_lo, 0)),
        ],
        out_specs=[
            pl.BlockSpec((d, tr), lambda i, st: (0, i)),
            pl.BlockSpec((d, LANES), lambda i, st: (0, 0)),
        ],
    )
    return pl.pallas_call(
        functools.partial(_w_in_prep_kernel, rank=rank),
        grid_spec=grid_spec,
        out_shape=[
            jax.ShapeDtypeStruct((d, n_blocks * tr), BF16),
            jax.ShapeDtypeStruct((d, LANES), BF16),
        ],
        compiler_params=_cparams(("arbitrary",)),
        name="w_in_prep",
    )(jnp.asarray([s // ROW_UNIT for s in starts], jnp.int32), wt, wt)


def _in_proj_kernel(x_ref, g_ref, w_ref, wu_ref, wlow_ref, wa2_ref, ba2_ref,
                    main_ref, u_ref, loga_ref, h0_ref, h1_ref, uacc_ref, *, n_main):
    i = pl.program_id(0)
    j = pl.program_id(1)
    chunk = x_ref.shape[0]

    def normalise_chunk(h_ref):
        x = x_ref[...]
        ms = jnp.mean(x * x, axis=-1, keepdims=True)
        rows = pl.ds(pl.multiple_of(j * chunk, chunk), chunk)
        h_ref[rows, :] = (x * lax.rsqrt(ms + NORM_EPS) * g_ref[...]).astype(BF16)

    def tail(h):
        a_low = _dot(h, wlow_ref[...])
        uacc = _dot(h, wu_ref[...])
        z = _dot(a_low, wa2_ref[...]) + ba2_ref[...]
        loga_ref[...] = _log_sigmoid(z) * (LOG2_E / GLA_GATE_TAU)
        rows, width = u_ref.shape[0], wu_ref.shape[1]
        for c in range(width // LANES):
            uacc_ref[c] = uacc[:, c * LANES:(c + 1) * LANES]
        for s in range(S5_STEPS):
            for c in range(width // LANES):
                lo = s * width + c * LANES
                u_ref[:, lo:lo + LANES] = uacc_ref[c, pl.ds(s, rows, stride=S5_STEPS), :].astype(BF16)

    @pl.when(i == 0)
    def _():
        normalise_chunk(h0_ref)

    for parity, (h_write, h_read) in enumerate(((h0_ref, h1_ref), (h1_ref, h0_ref))):
        active = (i > 0) & (i % 2 == parity)

        @pl.when(active & (j < n_main))
        def _(h_write=h_write, h_read=h_read):
            normalise_chunk(h_write)
            main_ref[...] = _dot(h_read[...], w_ref[...]).astype(BF16)

        @pl.when(active & (j == n_main))
        def _(h_write=h_write, h_read=h_read):
            normalise_chunk(h_write)
            tail(h_read[...])


def _in_proj(x2, g, w, w_low, wa2p, ba2, *, tm, tn, sw):
    n, d = x2.shape
    n_main = (w.shape[1] - sw) // tn
    hk = wa2p.shape[1]
    kern = functools.partial(_in_proj_kernel, n_main=n_main)
    nt = n // tm
    n_steps = n_main + 1
    assert tm % n_steps == 0
    chunk = tm // n_steps

    def prev(i):
        return jnp.maximum(i - 1, 0)

    return pl.pallas_call(
        kern,
        grid=(nt + 1, n_steps),
        in_specs=[
            pl.BlockSpec((chunk, d), lambda i, j: (jnp.minimum(i, nt - 1) * n_steps + j, 0)),
            pl.BlockSpec((1, d), lambda i, j: (0, 0)),
            pl.BlockSpec((d, tn), lambda i, j: (0, jnp.minimum(j, n_main - 1))),
            pl.BlockSpec((d, sw), lambda i, j: (0, n_main * tn // sw), pipeline_mode=pl.Buffered(1)),
            _resident(w_low.shape),
            _resident(wa2p.shape),
            _resident(ba2.shape),
        ],
        out_specs=[
            pl.BlockSpec((tm, tn), lambda i, j: (prev(i), jnp.where(i == 0, 0, jnp.minimum(j, n_main - 1)))),
            pl.BlockSpec((tm // S5_STEPS, S5_STEPS * sw), lambda i, j: (prev(i), 0)),
            pl.BlockSpec((tm, hk), lambda i, j: (prev(i), 0)),
        ],
        out_shape=[
            jax.ShapeDtypeStruct((n, n_main * tn), BF16),
            jax.ShapeDtypeStruct((n // S5_STEPS, S5_STEPS * sw), BF16),
            jax.ShapeDtypeStruct((n, hk), F32),
        ],
        scratch_shapes=[pltpu.VMEM((tm, d), BF16), pltpu.VMEM((tm, d), BF16),
                        pltpu.VMEM((sw // LANES, tm, LANES), F32)],
        compiler_params=_cparams(("arbitrary", "arbitrary")),
        name="in_proj",
    )(x2, g, w, w, w_low, wa2p, ba2)


def _kr_base(m):
    return GLA_CHUNK * (m * (m + 1) // 2)


def _gla_block(q_ref, k_ref, v_ref, loga_ref, ghn_ref, tri_ref,
               o_ref, s_ref, qd_ref, qb_ref, kb_ref, kr_ref, p_ref, *, n_sub, dk, dv):
    c = GLA_CHUNK
    shared = {}

    def decays():
        log_a = loga_ref[...]
        la_hi = log_a.astype(BF16)
        la_lo = (log_a - la_hi.astype(F32)).astype(BF16)
        tri = tri_ref[...]
        bc_all = _dot(tri, la_hi) + _dot(tri, la_lo)
        bl = [bc_all[(i + 1) * c - 1:(i + 1) * c, :] for i in range(n_sub)]
        bs = [jnp.zeros_like(bl[0])]
        for i in range(n_sub):
            bs.append(bs[-1] + bl[i])
        shared.update(bc_all=bc_all, bl=bl, bs=bs)

    def operands(i):
        bc_all, bl, bs = shared["bc_all"], shared["bl"], shared["bs"]
        btot = bs[n_sub]
        rows = slice(i * c, (i + 1) * c)
        b = bc_all[rows, :]
        q = q_ref[rows, :].astype(F32) * (dk ** -0.5)
        k = k_ref[rows, :].astype(F32)
        qd = q * jnp.exp2(b)
        ks = k * jnp.exp2(bl[i] - b)
        qd_ref[rows, :] = qd.astype(BF16)
        qb_ref[rows, :] = (qd * jnp.exp2(bs[i])).astype(BF16)
        kb_ref[rows, :] = (ks * jnp.exp2(btot - bs[i + 1])).astype(BF16)
        own = _kr_base(i) + i * c
        kr_ref[own:own + c, :] = (k * jnp.exp2(-b)).astype(BF16)
        for m in range(i + 1, n_sub):
            lo = _kr_base(m) + i * c
            kr_ref[lo:lo + c, :] = (ks * jnp.exp2(bs[m] - bs[i + 1])).astype(BF16)

    def scores(h):
        row = lax.broadcasted_iota(jnp.int32, (c, c), 0)
        col = lax.broadcasted_iota(jnp.int32, (c, c), 1)
        causal = row >= col
        ks_ = slice(h * dk, (h + 1) * dk)
        for m in range(n_sub):
            rows = slice(m * c, (m + 1) * c)
            keys = slice(_kr_base(m), _kr_base(m) + (m + 1) * c)
            s = lax.dot_general(qd_ref[rows, ks_], kr_ref[keys, ks_], (((1,), (1,)), ((), ())),
                                preferred_element_type=F32)
            if m > 0:
                p_ref[h, rows, 0:m * c] = s[:, 0:m * c].astype(BF16)
            p_ref[h, rows, m * c:(m + 1) * c] = jnp.where(causal, s[:, m * c:], 0.0).astype(BF16)

    def head(h):
        btot = shared["bs"][n_sub]
        ks_ = slice(h * dk, (h + 1) * dk)
        vs_ = slice(h * dv, (h + 1) * dv)
        v = v_ref[:, vs_]
        state = s_ref[h]
        shared["o", h] = _dot(p_ref[h], v) + _dot(qb_ref[:, ks_], state.astype(BF16))
        shared["state", h] = state

    def finish(h):
        btot = shared["bs"][n_sub]
        ks_ = slice(h * dk, (h + 1) * dk)
        vs_ = slice(h * dv, (h + 1) * dv)
        upd = lax.dot_general(kb_ref[:, ks_], v_ref[:, vs_], (((0,), (0,)), ((), ())),
                              preferred_element_type=F32)
        decay_col = jnp.transpose(jnp.broadcast_to(jnp.exp2(btot[:, ks_]), (dk, dk)))
        if dv > dk:
            decay_col = jnp.concatenate([decay_col] * (dv // dk), axis=1)
        s_ref[h] = decay_col * shared["state", h] + upd
        o = shared["o", h]
        o = o * lax.rsqrt(jnp.mean(o * o, axis=-1, keepdims=True) + NORM_EPS)
        o_ref[:, vs_] = (o * ghn_ref[:, vs_]).astype(BF16)

    stages = [functools.partial(scores, 0), functools.partial(scores, 1)]
    for h in range(GLA_HEADS):
        if h + 2 < GLA_HEADS:
            stages.append(functools.partial(scores, h + 2))
        stages.append(functools.partial(head, h))
        if h > 0:
            stages.append(functools.partial(finish, h - 1))
    stages.append(functools.partial(finish, GLA_HEADS - 1))
    return decays, [functools.partial(operands, i) for i in range(n_sub)], stages


def _gla_kernel(q_ref, k_ref, v_ref, loga_ref, ghn_ref, tri_ref,
                o_ref, s_ref, qd_ref, qb_ref, kb_ref, kr_ref, p_ref, *, tc, dk, dv):
    @pl.when(pl.program_id(1) == 0)
    def _():
        s_ref[...] = jnp.zeros_like(s_ref)
        p_ref[...] = jnp.zeros_like(p_ref)

    blocks = []
    for blk in range(q_ref.shape[0] // tc):
        rows = pl.ds(blk * tc, tc)
        blocks.append(_gla_block(
            q_ref.at[rows], k_ref.at[rows], v_ref.at[rows], loga_ref.at[rows], ghn_ref, tri_ref,
            o_ref.at[rows], s_ref, qd_ref.at[blk], qb_ref.at[blk], kb_ref.at[blk], kr_ref.at[blk],
            p_ref.at[blk], n_sub=tc // GLA_CHUNK, dk=dk, dv=dv))

    decays, operands, _ = blocks[0]
    decays()
    for stage in operands:
        stage()
    for blk, (_, _, heads) in enumerate(blocks):
        nxt = blocks[blk + 1] if blk + 1 < len(blocks) else None
        if nxt is not None:
            nxt[0]()
        per_stage = -(-len(nxt[1]) // len(heads)) if nxt is not None else 0
        for h, stage in enumerate(heads):
            stage()
            if nxt is not None:
                for operand_stage in nxt[1][h * per_stage:(h + 1) * per_stage]:
                    operand_stage()


def _gla(main3, loga3, ghn, tri, *, ts, tc, col_q, col_k, col_v, dk, dv):
    b, t, _ = main3.shape
    hk, hv = GLA_HEADS * dk, GLA_HEADS * dv
    kern = functools.partial(_gla_kernel, tc=tc, dk=dk, dv=dv)
    nb = ts // tc
    return pl.pallas_call(
        kern,
        grid=(b, t // ts),
        in_specs=[
            pl.BlockSpec((None, ts, hk), lambda i, j: (i, j, col_q // hk)),
            pl.BlockSpec((None, ts, hk), lambda i, j: (i, j, col_k // hk)),
            pl.BlockSpec((None, ts, hv), lambda i, j: (i, j, col_v // hv)),
            pl.BlockSpec((None, ts, hk), lambda i, j: (i, j, 0)),
            pl.BlockSpec((1, hv), lambda i, j: (0, 0)),
            pl.BlockSpec((tc, tc), lambda i, j: (0, 0)),
        ],
        out_specs=pl.BlockSpec((None, ts, hv), lambda i, j: (i, j, 0)),
        out_shape=jax.ShapeDtypeStruct((b, t, hv), BF16),
        scratch_shapes=[
            pltpu.VMEM((GLA_HEADS, dk, dv), F32),
            pltpu.VMEM((nb, tc, hk), BF16),
            pltpu.VMEM((nb, tc, hk), BF16),
            pltpu.VMEM((nb, tc, hk), BF16),
            pltpu.VMEM((nb, _kr_base(tc // GLA_CHUNK), hk), BF16),
            pltpu.VMEM((nb, GLA_HEADS, tc, tc), BF16),
        ],
        compiler_params=_cparams(("parallel", "arbitrary")),
        name="gla",
    )(main3, main3, main3, loga3, ghn, tri)


def _cmul(ar, ai, br, bi):
    return ar * br - ai * bi, ar * bi + ai * br


def _s5_tables_kernel(lr_r, li_r, ldt_r, lr_c, li_c, ldt_c, bre, bim, cre, cim,
                      m_ref, wst_ref, wout_ref, pw_ref, *, gc, p):
    L = S5_STEPS
    sw = S5_SLAB_GROUPS * gc
    sp = S5_SLAB_GROUPS * p

    def discretise(lr, li, ldt):
        dt = jnp.exp(ldt)
        mag = jnp.exp(lr * dt)
        return mag * jnp.cos(li * dt), mag * jnp.sin(li * dt)

    lr, li = lr_r[...], li_r[...]
    ar, ai = discretise(lr, li, ldt_r[...])
    den = lr * lr + li * li
    am1 = ar - 1.0
    f_re = (am1 * lr + ai * li) / den
    f_im = (ai * lr - am1 * li) / den

    rg = lax.broadcasted_iota(jnp.int32, (sw, sp), 0) // gc
    lg = lax.broadcasted_iota(jnp.int32, (sw, sp), 1) // p
    bmask = rg == lg
    br, bi = bre[...], bim[...]
    bbr = jnp.where(bmask, f_re * br - f_im * bi, 0.0)
    bbi = jnp.where(bmask, f_re * bi + f_im * br, 0.0)

    rg2 = lax.broadcasted_iota(jnp.int32, (sp, sw), 0) // p
    lg2 = lax.broadcasted_iota(jnp.int32, (sp, sw), 1) // gc
    cmask = rg2 == lg2
    cr = jnp.where(cmask, cre[...], 0.0)
    ci = jnp.where(cmask, cim[...], 0.0)

    pr, pi = [jnp.ones_like(ar)], [jnp.zeros_like(ai)]
    for _ in range(L):
        nr, ni = _cmul(pr[-1], pi[-1], ar, ai)
        pr.append(nr)
        pi.append(ni)

    def split(a):
        hi = a.astype(BF16)
        return hi, (a - hi.astype(F32)).astype(BF16)

    c_hi, c_lo = split(jnp.concatenate([cr, -ci], axis=0))

    kern = []
    for k in range(L):
        xr, xi = _cmul(bbr, bbi, pr[k], pi[k])
        j = L - 1 - k
        wst_ref[j * sw:(j + 1) * sw, 0:sp] = xr.astype(BF16)
        wst_ref[j * sw:(j + 1) * sw, sp:2 * sp] = xi.astype(BF16)
        x_hi, x_lo = split(jnp.concatenate([xr, xi], axis=1))
        kk = _dot(x_hi, c_hi) + (_dot(x_hi, c_lo) + _dot(x_lo, c_hi))
        kern.append(kk.astype(BF16))
    zero = jnp.zeros((sw, sw), BF16)
    for j in range(L):
        for i in range(L):
            m_ref[j * sw:(j + 1) * sw, i * sw:(i + 1) * sw] = kern[i - j] if i >= j else zero

    acr, aci = discretise(lr_c[...], li_c[...], ldt_c[...])
    qr, qi = acr, aci
    for i in range(L):
        zr, zi = _cmul(cr, ci, qr, qi)
        wout_ref[0:sp, i * sw:(i + 1) * sw] = zr.astype(BF16)
        wout_ref[sp:2 * sp, i * sw:(i + 1) * sw] = (-zi).astype(BF16)
        qr, qi = _cmul(qr, qi, acr, aci)

    a8r, a8i = pr[L], pi[L]
    tr, ti = a8r, a8i
    for i in range(SUBLANES):
        pw_ref[i:i + 1, 0:sp] = tr
        pw_ref[i:i + 1, sp:2 * sp] = ti
        tr, ti = _cmul(tr, ti, a8r, a8i)


def _s5_tables(lam_re, lam_im, log_dt, b_re, b_im, c_re, c_im):
    g, p = lam_re.shape
    gc = b_re.shape[2]
    ns = g // S5_SLAB_GROUPS
    sw, sp = S5_SLAB_GROUPS * gc, S5_SLAB_GROUPS * p
    lw = S5_STEPS * sw

    def rows(a):
        return a.reshape(ns, 1, sp)

    def cols(a):
        return a.reshape(ns, sp, 1)

    ldt = jnp.broadcast_to(log_dt[:, None], (g, p))

    def b_layout(b):
        bt = jnp.transpose(b, (0, 2, 1)).reshape(ns, sw, p)
        return jnp.tile(bt, (1, 1, S5_SLAB_GROUPS))

    def c_layout(c):
        ct = jnp.transpose(c.reshape(ns, S5_SLAB_GROUPS, gc, p), (0, 3, 1, 2)).reshape(ns, p, sw)
        return jnp.tile(ct, (1, S5_SLAB_GROUPS, 1))

    row_spec = pl.BlockSpec((None, 1, sp), lambda s: (s, 0, 0))
    col_spec = pl.BlockSpec((None, sp, 1), lambda s: (s, 0, 0))
    b_spec = pl.BlockSpec((None, sw, sp), lambda s: (s, 0, 0))
    c_spec = pl.BlockSpec((None, sp, sw), lambda s: (s, 0, 0))
    kern = functools.partial(_s5_tables_kernel, gc=gc, p=p)
    return pl.pallas_call(
        kern,
        grid=(ns,),
        in_specs=[row_spec, row_spec, row_spec, col_spec, col_spec, col_spec,
                  b_spec, b_spec, c_spec, c_spec],
        out_specs=[
            pl.BlockSpec((None, lw, lw), lambda s: (s, 0, 0)),
            pl.BlockSpec((None, lw, 2 * sp), lambda s: (s, 0, 0)),
            pl.BlockSpec((None, 2 * sp, lw), lambda s: (s, 0, 0)),
            pl.BlockSpec((None, SUBLANES, 2 * sp), lambda s: (s, 0, 0)),
        ],
        out_shape=[
            jax.ShapeDtypeStruct((ns, lw, lw), BF16),
            jax.ShapeDtypeStruct((ns, lw, 2 * sp), BF16),
            jax.ShapeDtypeStruct((ns, 2 * sp, lw), BF16),
            jax.ShapeDtypeStruct((ns, SUBLANES, 2 * sp), F32),
        ],
        compiler_params=_cparams(("parallel",)),
        name="s5_tables",
    )(rows(lam_re), rows(lam_im), rows(ldt), cols(lam_re), cols(lam_im), cols(ldt),
      b_layout(b_re), b_layout(b_im), c_layout(c_re), c_layout(c_im))


def _s5_kernel(u_ref, m_ref, wst_ref, wout_ref, pw_ref, d_ref, wglu_ref, bglu_ref,
               o_ref, carry_ref, sloc_ref, sprev_ref, y_ref, out_ref, *, n_slabs, width):
    L = S5_STEPS
    sw = LANES
    lw = L * sw
    rb = u_ref.shape[0]
    sp2 = sloc_ref.shape[2]
    sp = sp2 // 2

    @pl.when(pl.program_id(1) == 0)
    def _():
        carry_ref[...] = jnp.zeros_like(carry_ref)

    row_id = lax.broadcasted_iota(jnp.int32, (SUBLANES, sp2), 0)

    def slab_input(s):
        return jnp.concatenate(
            [u_ref[:, j * width + s * sw: j * width + (s + 1) * sw] for j in range(L)], axis=1)

    for s in range(n_slabs):
        sloc_ref[s] = _dot(slab_input(s), wst_ref[s])

    pws = [pw_ref[s] for s in range(n_slabs)]
    steps = [{sh: jnp.where(row_id >= sh, pws[s][sh - 1:sh, :], 0.0) for sh in (1, 2, 4)}
             for s in range(n_slabs)]
    carries = [carry_ref[s] for s in range(n_slabs)]
    for gi in range(rb // SUBLANES):
        grp = slice(gi * SUBLANES, (gi + 1) * SUBLANES)
        for s in range(n_slabs):
            pwr, pwi = pws[s][:, :sp], pws[s][:, sp:]
            carry = carries[s]
            x = sloc_ref[s, grp, :]
            for sh in (1, 2, 4):
                sft = pltpu.roll(x, sh, axis=0)
                mult = steps[s][sh]
                tr, ti = _cmul(sft[:, :sp], sft[:, sp:], mult[:, :sp], mult[:, sp:])
                x = x + jnp.concatenate([tr, ti], axis=1)
            cr, ci = _cmul(carry[:, :sp], carry[:, sp:], pwr, pwi)
            incl = x + jnp.concatenate([cr, ci], axis=1)
            sprev_ref[s, grp, :] = jnp.where(row_id == 0, carry, pltpu.roll(incl, 1, axis=0))
            carries[s] = incl[SUBLANES - 1:SUBLANES, :]
    for s in range(n_slabs):
        carry_ref[s] = carries[s]

    for s in range(n_slabs):
        u_slab = slab_input(s)
        y_inter = _dot(sprev_ref[s].astype(BF16), wout_ref[s])
        for nt in range(lw // MXU_TILE):
            hi = (nt + 1) * MXU_TILE
            cols = slice(nt * MXU_TILE, hi)
            y = y_inter[:, cols] + _dot(u_slab[:, 0:hi], m_ref[s, 0:hi, cols])
            for i in range(nt * MXU_TILE // sw, hi // sw):
                lo = i * sw - nt * MXU_TILE
                y_ref[:, i * width + s * sw: i * width + (s + 1) * sw] = y[:, lo:lo + sw]

    for i in range(L):
        cols = slice(i * width, (i + 1) * width)
        yi = y_ref[:, cols] + d_ref[...] * u_ref[:, cols].astype(F32)
        hh = _gelu_tanh(yi)
        gate = _sigmoid(_dot(hh.astype(BF16), wglu_ref[...]) + bglu_ref[...])
        val = hh * gate
        for c in range(width // LANES):
            out_ref[c, pl.ds(i, rb, stride=L), :] = val[:, c * LANES:(c + 1) * LANES]
    for c in range(width // LANES):
        o_ref[:, c * LANES:(c + 1) * LANES] = out_ref[c].astype(BF16)


def _s5(u3, m, wst, wout, pw, d_skip, wglu, bglu, *, rb):
    b, rows, lw = u3.shape
    ns = m.shape[0]
    width = lw // S5_STEPS
    sp2 = wst.shape[2]
    kern = functools.partial(_s5_kernel, n_slabs=ns, width=width)
    return pl.pallas_call(
        kern,
        grid=(b, rows // rb),
        in_specs=[
            pl.BlockSpec((None, rb, lw), lambda i, j: (i, j, 0)),
            _resident(m.shape), _resident(wst.shape), _resident(wout.shape), _resident(pw.shape),
            _resident(d_skip.shape), _resident(wglu.shape), _resident(bglu.shape),
        ],
        out_specs=pl.BlockSpec((None, rb * S5_STEPS, width), lambda i, j: (i, j, 0)),
        out_shape=jax.ShapeDtypeStruct((b, rows * S5_STEPS, width), BF16),
        scratch_shapes=[
            pltpu.VMEM((ns, 1, sp2), F32),
            pltpu.VMEM((ns, rb, sp2), F32),
            pltpu.VMEM((ns, rb, sp2), F32),
            pltpu.VMEM((rb, lw), F32),
            pltpu.VMEM((width // LANES, rb * S5_STEPS, LANES), F32),
        ],
        compiler_params=_cparams(("parallel", "arbitrary")),
        name="s5",
    )(u3, m, wst, wout, pw, d_skip, wglu, bglu)


def _mix_kernel(x_ref, ga_ref, gb_ref, r_ref, gla_ref, ob_ref, wa_ref, wb_ref, wo_ref, o_ref):
    r = r_ref[...].astype(F32)
    o_a = (r * _sigmoid(r) * gla_ref[...].astype(F32)).astype(BF16)
    a = _dot(o_a, wa_ref[...])
    b = _dot(ob_ref[...], wb_ref[...])
    mix = _sigmoid(ga_ref[...].astype(F32)) * a + _sigmoid(gb_ref[...].astype(F32)) * b
    o_ref[...] = x_ref[...] + _dot(mix.astype(BF16), wo_ref[...])


def _mix(x2, main, gla, o_b, w_a, w_b, w_o, *, tm, col_r):
    n, d = x2.shape
    hv = gla.shape[1]
    return pl.pallas_call(
        _mix_kernel,
        grid=(n // tm,),
        in_specs=[
            pl.BlockSpec((tm, d), lambda i: (i, 0)),
            pl.BlockSpec((tm, d), lambda i: (i, 0)),
            pl.BlockSpec((tm, d), lambda i: (i, 1)),
            pl.BlockSpec((tm, hv), lambda i: (i, col_r // hv)),
            pl.BlockSpec((tm, hv), lambda i: (i, 0)),
            pl.BlockSpec((tm, o_b.shape[1]), lambda i: (i, 0)),
            _resident(w_a.shape), _resident(w_b.shape), _resident(w_o.shape),
        ],
        out_specs=pl.BlockSpec((tm, d), lambda i: (i, 0)),
        out_shape=jax.ShapeDtypeStruct((n, d), F32),
        compiler_params=_cparams(("parallel",)),
        name="mix",
    )(x2, main, main, main, gla, o_b, w_a, w_b, w_o)


def _ffn_kernel(x_ref, g2_ref, wg_ref, wu_ref, wd_ref, gf_ref, o_ref, h_ref):
    j = pl.program_id(1)

    @pl.when(j == 0)
    def _():
        x = x_ref[...]
        ms = jnp.mean(x * x, axis=-1, keepdims=True)
        h_ref[...] = (x * lax.rsqrt(ms + NORM_EPS) * g2_ref[...]).astype(BF16)
        o_ref[...] = x

    h = h_ref[...]
    gate = _dot(h, wg_ref[...])
    up = _dot(h, wu_ref[...])
    act = (gate * _sigmoid(gate) * up).astype(BF16)
    o_ref[...] += _dot(act, wd_ref[...])

    @pl.when(j == pl.num_programs(1) - 1)
    def _():
        y = o_ref[...]
        ms = jnp.mean(y * y, axis=-1, keepdims=True)
        o_ref[...] = y * lax.rsqrt(ms + NORM_EPS) * gf_ref[...]


def _ffn(x1, g2, w_in, w_out, gf, *, tm, tf):
    n, d = x1.shape
    f = w_out.shape[0]
    nf = f // tf
    return pl.pallas_call(
        _ffn_kernel,
        grid=(n // tm, nf),
        in_specs=[
            pl.BlockSpec((tm, d), lambda i, j: (i, 0)),
            pl.BlockSpec((1, d), lambda i, j: (0, 0)),
            pl.BlockSpec((d, tf), lambda i, j: (0, j)),
            pl.BlockSpec((d, tf), lambda i, j: (0, j + nf)),
            pl.BlockSpec((tf, d), lambda i, j: (j, 0)),
            pl.BlockSpec((1, d), lambda i, j: (0, 0)),
        ],
        out_specs=pl.BlockSpec((tm, d), lambda i, j: (i, 0)),
        out_shape=jax.ShapeDtypeStruct((n, d), F32),
        scratch_shapes=[pltpu.VMEM((tm, d), BF16)],
        compiler_params=_cparams(("parallel", "arbitrary")),
        name="ffn",
    )(x1, g2, w_in, w_in, w_out, gf)


def _tile(n, pref):
    t = min(n, pref)
    assert n % t == 0, (n, t)
    return t


def _layer(x2, bsz, seq, norm1_g, w_in, w_a2, b_a2, gla_norm_g, lam_re, lam_im, log_dt,
           s5_b_re, s5_b_im, s5_c_re, s5_c_im, s5_d, w_glu, b_glu,
           w_branch_a, w_branch_b, w_out, norm2_g, w_ffn_in, w_ffn_out, final_g):
    n, d = x2.shape
    hk = w_a2.shape[1]
    hv = gla_norm_g.shape[0]
    rank = w_a2.shape[0]
    sw = w_glu.shape[0]
    dk, dv = hk // GLA_HEADS, hv // GLA_HEADS

    o_q, o_k, o_v, o_r = 0, hk, 2 * hk, 2 * hk + hv
    o_al = 2 * hk + 2 * hv
    o_u = o_al + rank
    o_ga = o_u + sw
    o_gb = o_ga + d
    assert o_gb + d == w_in.shape[1]
    tr = sw
    segments = ((o_ga, 2 * d), (o_v, 2 * hv), (o_q, 2 * hk), (o_u, sw))
    starts = [lo + k for lo, width in segments for k in range(0, width, tr)]
    w_re, w_low = _w_in_prep(jnp.swapaxes(w_in, 0, 1), starts=starts, low_lo=o_al, rank=rank, tr=tr)
    wa2p = jnp.pad(w_a2, ((0, LANES - rank), (0, 0)))
    col_v, col_r, col_q, col_k = 2 * d, 2 * d + hv, 2 * d + 2 * hv, 2 * d + 2 * hv + hk

    main, u, loga = _in_proj(x2, norm1_g.reshape(1, d), w_re, w_low, wa2p, b_a2.reshape(1, hk),
                             tm=_tile(n, 1024), tn=1024, sw=sw)

    ts = _tile(seq, 512)
    tc = _tile(ts, 256)
    blk = jnp.arange(tc) // GLA_CHUNK
    tri = ((blk[:, None] == blk[None, :]) & (jnp.arange(tc)[:, None] >= jnp.arange(tc)[None, :]))
    gla = _gla(main.reshape(bsz, seq, -1), loga.reshape(bsz, seq, hk), gla_norm_g.reshape(1, hv),
               tri.astype(BF16), ts=ts, tc=tc, col_q=col_q, col_k=col_k, col_v=col_v, dk=dk, dv=dv)

    m, wst, wout, pw = _s5_tables(lam_re, lam_im, log_dt, s5_b_re, s5_b_im, s5_c_re, s5_c_im)
    rows = seq // S5_STEPS
    o_b = _s5(u.reshape(bsz, rows, S5_STEPS * sw), m, wst, wout, pw, s5_d.reshape(1, sw),
              w_glu.astype(BF16), b_glu.reshape(1, sw), rb=_tile(rows, 256))

    x1 = _mix(x2, main, gla.reshape(n, hv), o_b.reshape(n, sw), w_branch_a.astype(BF16),
              w_branch_b.astype(BF16), w_out.astype(BF16), tm=_tile(n, 512), col_r=col_r)
    return _ffn(x1, norm2_g.reshape(1, d), w_ffn_in.astype(BF16), w_ffn_out.astype(BF16),
                final_g.reshape(1, d), tm=_tile(n, 1024), tf=512)


def kernel(x, norm1_g, w_in, w_a2, b_a2, gla_norm_g, lam_re, lam_im, log_dt, s5_b_re, s5_b_im,
           s5_c_re, s5_c_im, s5_d, w_glu, b_glu, w_branch_a, w_branch_b, w_out, norm2_g,
           w_ffn_in, w_ffn_out, final_norm_g):
    bsz, seq, d = x.shape
    assert norm1_g.shape[0] == 1, "single-layer block"
    out = _layer(x.reshape(bsz * seq, d), bsz, seq, norm1_g[0], w_in[0], w_a2[0], b_a2[0],
                 gla_norm_g[0], lam_re[0], lam_im[0], log_dt[0], s5_b_re[0], s5_b_im[0],
                 s5_c_re[0], s5_c_im[0], s5_d[0], w_glu[0], b_glu[0], w_branch_a[0],
                 w_branch_b[0], w_out[0], norm2_g[0], w_ffn_in[0], w_ffn_out[0], final_norm_g[0])
    return out.reshape(bsz, seq, d)
```

```python
import functools
import math

import jax
import jax.numpy as jnp
from jax import lax
from jax.experimental import pallas as pl
from jax.experimental.pallas import tpu as pltpu

F32 = jnp.float32
BF16 = jnp.bfloat16

NORM_EPS = 1e-6
GLA_HEADS = 4
GLA_CHUNK = 32
GLA_GATE_TAU = 16.0
GLA_SCORES_AHEAD = 2
LOG2_E = math.log2(math.e)
LANES = 128
SUBLANES = 8
MXU_TILE = 256
ROW_UNIT = 16
S5_STEPS = 8
S5_SLAB_GROUPS = 8
VMEM_LIMIT = 56 * 1024 * 1024


def _sigmoid(x):
    return 1.0 / (1.0 + jnp.exp(-x))


def _log_sigmoid(x):
    return jnp.minimum(x, 0.0) - jnp.log(1.0 + jnp.exp(-jnp.abs(x)))


def _gelu_tanh(x):
    c = math.sqrt(2.0 / math.pi)
    return 0.5 * x * (1.0 + jnp.tanh(c * (x + 0.044715 * (x * x * x))))


def _dot(a, b):
    return jnp.dot(a, b, preferred_element_type=F32)


def _cparams(sem):
    return pltpu.CompilerParams(dimension_semantics=sem, vmem_limit_bytes=VMEM_LIMIT)


def _resident(shape):
    zeros = (0,) * len(shape)
    return pl.BlockSpec(shape, lambda *_: zeros, pipeline_mode=pl.Buffered(1))


def _w_in_prep_kernel(starts_ref, w_ref, wlow_ref, out_ref, low_ref, *, rank):
    del starts_ref
    out_ref[...] = jnp.transpose(w_ref[...]).astype(BF16)

    @pl.when(pl.program_id(0) == 0)
    def _():
        lane = lax.broadcasted_iota(jnp.int32, low_ref.shape, 1)
        low_ref[...] = jnp.where(lane < rank, jnp.transpose(wlow_ref[...]), 0.0).astype(BF16)


def _w_in_prep(wt, *, starts, low_lo, rank, tr):
    _, d = wt.shape
    assert all(s % ROW_UNIT == 0 for s in starts) and low_lo % ROW_UNIT == 0
    n_blocks = len(starts)
    grid_spec = pltpu.PrefetchScalarGridSpec(
        num_scalar_prefetch=1,
        grid=(n_blocks,),
        in_specs=[
            pl.BlockSpec((pl.Element(tr), pl.Element(d)), lambda i, st: (st[i] * ROW_UNIT, 0)),
            pl.BlockSpec((pl.Element(LANES), pl.Element(d)), lambda i, st: (low_lo, 0)),
        ],
        out_specs=[
            pl.BlockSpec((d, tr), lambda i, st: (0, i)),
            pl.BlockSpec((d, LANES), lambda i, st: (0, 0)),
        ],
    )
    return pl.pallas_call(
        functools.partial(_w_in_prep_kernel, rank=rank),
        grid_spec=grid_spec,
        out_shape=[
            jax.ShapeDtypeStruct((d, n_blocks * tr), BF16),
            jax.ShapeDtypeStruct((d, LANES), BF16),
        ],
        compiler_params=_cparams(("arbitrary",)),
        name="w_in_prep",
    )(jnp.asarray([s // ROW_UNIT for s in starts], jnp.int32), wt, wt)


def _in_proj_kernel(x_ref, g_ref, w_ref, wu_ref, wlow_ref, wa2_ref, ba2_ref,
                    main_ref, u_ref, loga_ref, h0_ref, h1_ref, uacc_ref, *, n_main):
    i = pl.program_id(0)
    j = pl.program_id(1)
    chunk = x_ref.shape[0]

    def normalise_chunk(h_ref):
        x = x_ref[...]
        ms = jnp.mean(x * x, axis=-1, keepdims=True)
        rows = pl.ds(pl.multiple_of(j * chunk, chunk), chunk)
        h_ref[rows, :] = (x * lax.rsqrt(ms + NORM_EPS) * g_ref[...]).astype(BF16)

    def tail(h):
        a_low = _dot(h, wlow_ref[...])
        uacc = _dot(h, wu_ref[...])
        z = _dot(a_low, wa2_ref[...]) + ba2_ref[...]
        loga_ref[...] = _log_sigmoid(z) * (LOG2_E / GLA_GATE_TAU)
        rows, width = u_ref.shape[0], wu_ref.shape[1]
        for c in range(width // LANES):
            uacc_ref[c] = uacc[:, c * LANES:(c + 1) * LANES]
        for s in range(S5_STEPS):
            for c in range(width // LANES):
                lo = s * width + c * LANES
                u_ref[:, lo:lo + LANES] = uacc_ref[c, pl.ds(s, rows, stride=S5_STEPS), :].astype(BF16)

    @pl.when(i == 0)
    def _():
        normalise_chunk(h0_ref)

    for parity, (h_write, h_read) in enumerate(((h0_ref, h1_ref), (h1_ref, h0_ref))):
        active = (i > 0) & (i % 2 == parity)

        @pl.when(active & (j < n_main))
        def _(h_write=h_write, h_read=h_read):
            normalise_chunk(h_write)
            main_ref[...] = _dot(h_read[...], w_ref[...]).astype(BF16)

        @pl.when(active & (j == n_main))
        def _(h_write=h_write, h_read=h_read):
            normalise_chunk(h_write)
            tail(h_read[...])


def _in_proj(x2, g, w, w_low, wa2p, ba2, *, tm, tn, sw):
    n, d = x2.shape
    n_main = (w.shape[1] - sw) // tn
    hk = wa2p.shape[1]
    kern = functools.partial(_in_proj_kernel, n_main=n_main)
    nt = n // tm
    n_steps = n_main + 1
    assert tm % n_steps == 0
    chunk = tm // n_steps

    def prev(i):
        return jnp.maximum(i - 1, 0)

    return pl.pallas_call(
        kern,
        grid=(nt + 1, n_steps),
        in_specs=[
            pl.BlockSpec((chunk, d), lambda i, j: (jnp.minimum(i, nt - 1) * n_steps + j, 0)),
            pl.BlockSpec((1, d), lambda i, j: (0, 0)),
            pl.BlockSpec((d, tn), lambda i, j: (0, jnp.minimum(j, n_main - 1))),
            pl.BlockSpec((d, sw), lambda i, j: (0, n_main * tn // sw), pipeline_mode=pl.Buffered(1)),
            _resident(w_low.shape),
            _resident(wa2p.shape),
            _resident(ba2.shape),
        ],
        out_specs=[
            pl.BlockSpec((tm, tn), lambda i, j: (prev(i), jnp.where(i == 0, 0, jnp.minimum(j, n_main - 1)))),
            pl.BlockSpec((tm // S5_STEPS, S5_STEPS * sw), lambda i, j: (prev(i), 0)),
            pl.BlockSpec((tm, hk), lambda i, j: (prev(i), 0)),
        ],
        out_shape=[
            jax.ShapeDtypeStruct((n, n_main * tn), BF16),
            jax.ShapeDtypeStruct((n // S5_STEPS, S5_STEPS * sw), BF16),
            jax.ShapeDtypeStruct((n, hk), F32),
        ],
        scratch_shapes=[pltpu.VMEM((tm, d), BF16), pltpu.VMEM((tm, d), BF16),
                        pltpu.VMEM((sw // LANES, tm, LANES), F32)],
        compiler_params=_cparams(("arbitrary", "arbitrary")),
        name="in_proj",
    )(x2, g, w, w, w_low, wa2p, ba2)


def _kr_base(m):
    return GLA_CHUNK * (m * (m + 1) // 2)


def _gla_block(q_ref, k_ref, v_ref, loga_ref, ghn_ref, tri_ref,
               o_ref, s_ref, qd_ref, qb_ref, kb_ref, kr_ref, p_ref, *, n_sub, dk, dv):
    c = GLA_CHUNK
    shared = {}

    def decays():
        log_a = loga_ref[...]
        la_hi = log_a.astype(BF16)
        la_lo = (log_a - la_hi.astype(F32)).astype(BF16)
        tri = tri_ref[...]
        bc_all = _dot(tri, la_hi) + _dot(tri, la_lo)
        bl = [bc_all[(i + 1) * c - 1:(i + 1) * c, :] for i in range(n_sub)]
        bs = [jnp.zeros_like(bl[0])]
        for i in range(n_sub):
            bs.append(bs[-1] + bl[i])
        shared.update(bc_all=bc_all, bl=bl, bs=bs)

    def operands(i):
        bc_all, bl, bs = shared["bc_all"], shared["bl"], shared["bs"]
        btot = bs[n_sub]
        rows = slice(i * c, (i + 1) * c)
        b = bc_all[rows, :]
        q = q_ref[rows, :].astype(F32) * (dk ** -0.5)
        k = k_ref[rows, :].astype(F32)
        qd = q * jnp.exp2(b)
        ks = k * jnp.exp2(bl[i] - b)
        qd_ref[rows, :] = qd.astype(BF16)
        qb_ref[rows, :] = (qd * jnp.exp2(bs[i])).astype(BF16)
        kb_ref[rows, :] = (ks * jnp.exp2(btot - bs[i + 1])).astype(BF16)
        own = _kr_base(i) + i * c
        kr_ref[own:own + c, :] = (k * jnp.exp2(-b)).astype(BF16)
        for m in range(i + 1, n_sub):
            lo = _kr_base(m) + i * c
            kr_ref[lo:lo + c, :] = (ks * jnp.exp2(bs[m] - bs[i + 1])).astype(BF16)

    def scores(h):
        row = lax.broadcasted_iota(jnp.int32, (c, c), 0)
        col = lax.broadcasted_iota(jnp.int32, (c, c), 1)
        causal = row >= col
        ks_ = slice(h * dk, (h + 1) * dk)
        for m in range(n_sub):
            rows = slice(m * c, (m + 1) * c)
            keys = slice(_kr_base(m), _kr_base(m) + (m + 1) * c)
            s = lax.dot_general(qd_ref[rows, ks_], kr_ref[keys, ks_], (((1,), (1,)), ((), ())),
                                preferred_element_type=F32)
            if m > 0:
                p_ref[h, rows, 0:m * c] = s[:, 0:m * c].astype(BF16)
            p_ref[h, rows, m * c:(m + 1) * c] = jnp.where(causal, s[:, m * c:], 0.0).astype(BF16)

    def head(h):
        btot = shared["bs"][n_sub]
        ks_ = slice(h * dk, (h + 1) * dk)
        vs_ = slice(h * dv, (h + 1) * dv)
        v = v_ref[:, vs_]
        state = s_ref[h]
        shared["o", h] = _dot(p_ref[h], v) + _dot(qb_ref[:, ks_], state.astype(BF16))
        shared["state", h] = state

    def finish(h):
        btot = shared["bs"][n_sub]
        ks_ = slice(h * dk, (h + 1) * dk)
        vs_ = slice(h * dv, (h + 1) * dv)
        upd = lax.dot_general(kb_ref[:, ks_], v_ref[:, vs_], (((0,), (0,)), ((), ())),
                              preferred_element_type=F32)
        decay_col = jnp.transpose(jnp.broadcast_to(jnp.exp2(btot[:, ks_]), (dk, dk)))
        if dv > dk:
            decay_col = jnp.concatenate([decay_col] * (dv // dk), axis=1)
        s_ref[h] = decay_col * shared["state", h] + upd
        o = shared["o", h]
        o = o * lax.rsqrt(jnp.mean(o * o, axis=-1, keepdims=True) + NORM_EPS)
        o_ref[:, vs_] = (o * ghn_ref[:, vs_]).astype(BF16)

    stages = [functools.partial(scores, h) for h in range(GLA_SCORES_AHEAD)]
    for h in range(GLA_HEADS):
        if h + GLA_SCORES_AHEAD < GLA_HEADS:
            stages.append(functools.partial(scores, h + GLA_SCORES_AHEAD))
        stages.append(functools.partial(head, h))
        if h > 0:
            stages.append(functools.partial(finish, h - 1))
    stages.append(functools.partial(finish, GLA_HEADS - 1))
    return decays, [functools.partial(operands, i) for i in range(n_sub)], stages


def _gla_kernel(q_ref, k_ref, v_ref, loga_ref, ghn_ref, tri_ref,
                o_ref, s_ref, qd_ref, qb_ref, kb_ref, kr_ref, p_ref, *, tc, dk, dv):
    @pl.when(pl.program_id(1) == 0)
    def _():
        s_ref[...] = jnp.zeros_like(s_ref)
        p_ref[...] = jnp.zeros_like(p_ref)

    blocks = []
    for blk in range(q_ref.shape[0] // tc):
        rows = pl.ds(blk * tc, tc)
        blocks.append(_gla_block(
            q_ref.at[rows], k_ref.at[rows], v_ref.at[rows], loga_ref.at[rows], ghn_ref, tri_ref,
            o_ref.at[rows], s_ref, qd_ref.at[blk], qb_ref.at[blk], kb_ref.at[blk], kr_ref.at[blk],
            p_ref.at[blk], n_sub=tc // GLA_CHUNK, dk=dk, dv=dv))

    decays, operands, _ = blocks[0]
    decays()
    for stage in operands:
        stage()
    for blk, (_, _, heads) in enumerate(blocks):
        nxt = blocks[blk + 1] if blk + 1 < len(blocks) else None
        if nxt is not None:
            nxt[0]()
        per_stage = -(-len(nxt[1]) // len(heads)) if nxt is not None else 0
        for h, stage in enumerate(heads):
            stage()
            if nxt is not None:
                for operand_stage in nxt[1][h * per_stage:(h + 1) * per_stage]:
                    operand_stage()


def _gla(main3, loga3, ghn, tri, *, ts, tc, col_q, col_k, col_v, dk, dv):
    b, t, _ = main3.shape
    hk, hv = GLA_HEADS * dk, GLA_HEADS * dv
    kern = functools.partial(_gla_kernel, tc=tc, dk=dk, dv=dv)
    nb = ts // tc
    return pl.pallas_call(
        kern,
        grid=(b, t // ts),
        in_specs=[
            pl.BlockSpec((None, ts, hk), lambda i, j: (i, j, col_q // hk)),
            pl.BlockSpec((None, ts, hk), lambda i, j: (i, j, col_k // hk)),
            pl.BlockSpec((None, ts, hv), lambda i, j: (i, j, col_v // hv)),
            pl.BlockSpec((None, ts, hk), lambda i, j: (i, j, 0)),
            pl.BlockSpec((1, hv), lambda i, j: (0, 0)),
            pl.BlockSpec((tc, tc), lambda i, j: (0, 0)),
        ],
        out_specs=pl.BlockSpec((None, ts, hv), lambda i, j: (i, j, 0)),
        out_shape=jax.ShapeDtypeStruct((b, t, hv), BF16),
        scratch_shapes=[
            pltpu.VMEM((GLA_HEADS, dk, dv), F32),
            pltpu.VMEM((nb, tc, hk), BF16),
            pltpu.VMEM((nb, tc, hk), BF16),
            pltpu.VMEM((nb, tc, hk), BF16),
            pltpu.VMEM((nb, _kr_base(tc // GLA_CHUNK), hk), BF16),
            pltpu.VMEM((nb, GLA_HEADS, tc, tc), BF16),
        ],
        compiler_params=_cparams(("parallel", "arbitrary")),
        name="gla",
    )(main3, main3, main3, loga3, ghn, tri)


def _cmul(ar, ai, br, bi):
    return ar * br - ai * bi, ar * bi + ai * br


def _s5_tables_kernel(lr_r, li_r, ldt_r, lr_c, li_c, ldt_c, bre, bim, cre, cim,
                      m_ref, wst_ref, wout_ref, pw_ref, *, gc, p):
    L = S5_STEPS
    sw = S5_SLAB_GROUPS * gc
    sp = S5_SLAB_GROUPS * p

    def discretise(lr, li, ldt):
        dt = jnp.exp(ldt)
        mag = jnp.exp(lr * dt)
        return mag * jnp.cos(li * dt), mag * jnp.sin(li * dt)

    lr, li = lr_r[...], li_r[...]
    ar, ai = discretise(lr, li, ldt_r[...])
    den = lr * lr + li * li
    am1 = ar - 1.0
    f_re = (am1 * lr + ai * li) / den
    f_im = (ai * lr - am1 * li) / den

    rg = lax.broadcasted_iota(jnp.int32, (sw, sp), 0) // gc
    lg = lax.broadcasted_iota(jnp.int32, (sw, sp), 1) // p
    bmask = rg == lg
    br, bi = bre[...], bim[...]
    bbr = jnp.where(bmask, f_re * br - f_im * bi, 0.0)
    bbi = jnp.where(bmask, f_re * bi + f_im * br, 0.0)

    rg2 = lax.broadcasted_iota(jnp.int32, (sp, sw), 0) // p
    lg2 = lax.broadcasted_iota(jnp.int32, (sp, sw), 1) // gc
    cmask = rg2 == lg2
    cr = jnp.where(cmask, cre[...], 0.0)
    ci = jnp.where(cmask, cim[...], 0.0)

    pr, pi = [jnp.ones_like(ar)], [jnp.zeros_like(ai)]
    for _ in range(L):
        nr, ni = _cmul(pr[-1], pi[-1], ar, ai)
        pr.append(nr)
        pi.append(ni)

    def split(a):
        hi = a.astype(BF16)
        return hi, (a - hi.astype(F32)).astype(BF16)

    c_hi, c_lo = split(jnp.concatenate([cr, -ci], axis=0))

    kern = []
    for k in range(L):
        xr, xi = _cmul(bbr, bbi, pr[k], pi[k])
        j = L - 1 - k
        wst_ref[j * sw:(j + 1) * sw, 0:sp] = xr.astype(BF16)
        wst_ref[j * sw:(j + 1) * sw, sp:2 * sp] = xi.astype(BF16)
        x_hi, x_lo = split(jnp.concatenate([xr, xi], axis=1))
        kk = _dot(x_hi, c_hi) + (_dot(x_hi, c_lo) + _dot(x_lo, c_hi))
        kern.append(kk.astype(BF16))
    zero = jnp.zeros((sw, sw), BF16)
    for j in range(L):
        for i in range(L):
            m_ref[j * sw:(j + 1) * sw, i * sw:(i + 1) * sw] = kern[i - j] if i >= j else zero

    acr, aci = discretise(lr_c[...], li_c[...], ldt_c[...])
    qr, qi = acr, aci
    for i in range(L):
        zr, zi = _cmul(cr, ci, qr, qi)
        wout_ref[0:sp, i * sw:(i + 1) * sw] = zr.astype(BF16)
        wout_ref[sp:2 * sp, i * sw:(i + 1) * sw] = (-zi).astype(BF16)
        qr, qi = _cmul(qr, qi, acr, aci)

    a8r, a8i = pr[L], pi[L]
    tr, ti = a8r, a8i
    for i in range(SUBLANES):
        pw_ref[i:i + 1, 0:sp] = tr
        pw_ref[i:i + 1, sp:2 * sp] = ti
        tr, ti = _cmul(tr, ti, a8r, a8i)


def _s5_tables(lam_re, lam_im, log_dt, b_re, b_im, c_re, c_im):
    g, p = lam_re.shape
    gc = b_re.shape[2]
    ns = g // S5_SLAB_GROUPS
    sw, sp = S5_SLAB_GROUPS * gc, S5_SLAB_GROUPS * p
    lw = S5_STEPS * sw

    def rows(a):
        return a.reshape(ns, 1, sp)

    def cols(a):
        return a.reshape(ns, sp, 1)

    ldt = jnp.broadcast_to(log_dt[:, None], (g, p))

    def b_layout(b):
        bt = jnp.transpose(b, (0, 2, 1)).reshape(ns, sw, p)
        return jnp.tile(bt, (1, 1, S5_SLAB_GROUPS))

    def c_layout(c):
        ct = jnp.transpose(c.reshape(ns, S5_SLAB_GROUPS, gc, p), (0, 3, 1, 2)).reshape(ns, p, sw)
        return jnp.tile(ct, (1, S5_SLAB_GROUPS, 1))

    row_spec = pl.BlockSpec((None, 1, sp), lambda s: (s, 0, 0))
    col_spec = pl.BlockSpec((None, sp, 1), lambda s: (s, 0, 0))
    b_spec = pl.BlockSpec((None, sw, sp), lambda s: (s, 0, 0))
    c_spec = pl.BlockSpec((None, sp, sw), lambda s: (s, 0, 0))
    kern = functools.partial(_s5_tables_kernel, gc=gc, p=p)
    return pl.pallas_call(
        kern,
        grid=(ns,),
        in_specs=[row_spec, row_spec, row_spec, col_spec, col_spec, col_spec,
                  b_spec, b_spec, c_spec, c_spec],
        out_specs=[
            pl.BlockSpec((None, lw, lw), lambda s: (s, 0, 0)),
            pl.BlockSpec((None, lw, 2 * sp), lambda s: (s, 0, 0)),
            pl.BlockSpec((None, 2 * sp, lw), lambda s: (s, 0, 0)),
            pl.BlockSpec((None, SUBLANES, 2 * sp), lambda s: (s, 0, 0)),
        ],
        out_shape=[
            jax.ShapeDtypeStruct((ns, lw, lw), BF16),
            jax.ShapeDtypeStruct((ns, lw, 2 * sp), BF16),
            jax.ShapeDtypeStruct((ns, 2 * sp, lw), BF16),
            jax.ShapeDtypeStruct((ns, SUBLANES, 2 * sp), F32),
        ],
        compiler_params=_cparams(("parallel",)),
        name="s5_tables",
    )(rows(lam_re), rows(lam_im), rows(ldt), cols(lam_re), cols(lam_im), cols(ldt),
      b_layout(b_re), b_layout(b_im), c_layout(c_re), c_layout(c_im))


def _s5_kernel(u_ref, m_ref, wst_ref, wout_ref, pw_ref, d_ref, wglu_ref, bglu_ref,
               o_ref, carry_ref, sloc_ref, sprev_ref, y_ref, out_ref, *, n_slabs, width):
    L = S5_STEPS
    sw = LANES
    lw = L * sw
    rb = u_ref.shape[0]
    sp2 = sloc_ref.shape[2]
    sp = sp2 // 2

    @pl.when(pl.program_id(1) == 0)
    def _():
        carry_ref[...] = jnp.zeros_like(carry_ref)

    row_id = lax.broadcasted_iota(jnp.int32, (SUBLANES, sp2), 0)

    def slab_input(s):
        return jnp.concatenate(
            [u_ref[:, j * width + s * sw: j * width + (s + 1) * sw] for j in range(L)], axis=1)

    for s in range(n_slabs):
        sloc_ref[s] = _dot(slab_input(s), wst_ref[s])

    pws = [pw_ref[s] for s in range(n_slabs)]
    steps = [{sh: jnp.where(row_id >= sh, pws[s][sh - 1:sh, :], 0.0) for sh in (1, 2, 4)}
             for s in range(n_slabs)]
    carries = [carry_ref[s] for s in range(n_slabs)]

    def scan_group(slabs, gi):
        grp = slice(gi * SUBLANES, (gi + 1) * SUBLANES)
        for s in slabs:
            pwr, pwi = pws[s][:, :sp], pws[s][:, sp:]
            carry = carries[s]
            x = sloc_ref[s, grp, :]
            for sh in (1, 2, 4):
                sft = pltpu.roll(x, sh, axis=0)
                mult = steps[s][sh]
                tr, ti = _cmul(sft[:, :sp], sft[:, sp:], mult[:, :sp], mult[:, sp:])
                x = x + jnp.concatenate([tr, ti], axis=1)
            cr, ci = _cmul(carry[:, :sp], carry[:, sp:], pwr, pwi)
            incl = x + jnp.concatenate([cr, ci], axis=1)
            sprev_ref[s, grp, :] = jnp.where(row_id == 0, carry, pltpu.roll(incl, 1, axis=0))
            carries[s] = incl[SUBLANES - 1:SUBLANES, :]

    def y_tile(s, nt):
        hi = (nt + 1) * MXU_TILE
        cols = slice(nt * MXU_TILE, hi)
        y = (_dot(sprev_ref[s].astype(BF16), wout_ref[s, :, cols])
             + _dot(slab_input(s)[:, 0:hi], m_ref[s, 0:hi, cols]))
        for i in range(nt * MXU_TILE // sw, hi // sw):
            lo = i * sw - nt * MXU_TILE
            y_ref[:, i * width + s * sw: i * width + (s + 1) * sw] = y[:, lo:lo + sw]

    n_groups = rb // SUBLANES
    first, second = tuple(range(n_slabs // 2)), tuple(range(n_slabs // 2, n_slabs))
    for gi in range(n_groups):
        scan_group(first, gi)
    tiles = [(s, nt) for s in first for nt in range(lw // MXU_TILE)]
    every = n_groups // len(tiles)
    for gi in range(n_groups):
        scan_group(second, gi)
        if every and (gi + 1) % every == 0 and tiles:
            y_tile(*tiles.pop(0))
    for s, nt in tiles + [(s, nt) for s in second for nt in range(lw // MXU_TILE)]:
        y_tile(s, nt)
    for s in range(n_slabs):
        carry_ref[s] = carries[s]

    for i in range(L):
        cols = slice(i * width, (i + 1) * width)
        yi = y_ref[:, cols] + d_ref[...] * u_ref[:, cols].astype(F32)
        hh = _gelu_tanh(yi)
        gate = _sigmoid(_dot(hh.astype(BF16), wglu_ref[...]) + bglu_ref[...])
        val = hh * gate
        for c in range(width // LANES):
            out_ref[c, pl.ds(i, rb, stride=L), :] = val[:, c * LANES:(c + 1) * LANES]
    for c in range(width // LANES):
        o_ref[:, c * LANES:(c + 1) * LANES] = out_ref[c].astype(BF16)


def _s5(u3, m, wst, wout, pw, d_skip, wglu, bglu, *, rb):
    b, rows, lw = u3.shape
    ns = m.shape[0]
    width = lw // S5_STEPS
    sp2 = wst.shape[2]
    kern = functools.partial(_s5_kernel, n_slabs=ns, width=width)
    return pl.pallas_call(
        kern,
        grid=(b, rows // rb),
        in_specs=[
            pl.BlockSpec((None, rb, lw), lambda i, j: (i, j, 0)),
            _resident(m.shape), _resident(wst.shape), _resident(wout.shape), _resident(pw.shape),
            _resident(d_skip.shape), _resident(wglu.shape), _resident(bglu.shape),
        ],
        out_specs=pl.BlockSpec((None, rb * S5_STEPS, width), lambda i, j: (i, j, 0)),
        out_shape=jax.ShapeDtypeStruct((b, rows * S5_STEPS, width), BF16),
        scratch_shapes=[
            pltpu.VMEM((ns, 1, sp2), F32),
            pltpu.VMEM((ns, rb, sp2), F32),
            pltpu.VMEM((ns, rb, sp2), F32),
            pltpu.VMEM((rb, lw), F32),
            pltpu.VMEM((width // LANES, rb * S5_STEPS, LANES), F32),
        ],
        compiler_params=_cparams(("parallel", "arbitrary")),
        name="s5",
    )(u3, m, wst, wout, pw, d_skip, wglu, bglu)


def _mix_kernel(x_ref, ga_ref, gb_ref, r_ref, gla_ref, ob_ref, wa_ref, wb_ref, wo_ref, o_ref):
    r = r_ref[...].astype(F32)
    o_a = (r * _sigmoid(r) * gla_ref[...].astype(F32)).astype(BF16)
    a = _dot(o_a, wa_ref[...])
    b = _dot(ob_ref[...], wb_ref[...])
    mix = _sigmoid(ga_ref[...].astype(F32)) * a + _sigmoid(gb_ref[...].astype(F32)) * b
    o_ref[...] = x_ref[...] + _dot(mix.astype(BF16), wo_ref[...])


def _mix(x2, main, gla, o_b, w_a, w_b, w_o, *, tm, col_r):
    n, d = x2.shape
    hv = gla.shape[1]
    return pl.pallas_call(
        _mix_kernel,
        grid=(n // tm,),
        in_specs=[
            pl.BlockSpec((tm, d), lambda i: (i, 0)),
            pl.BlockSpec((tm, d), lambda i: (i, 0)),
            pl.BlockSpec((tm, d), lambda i: (i, 1)),
            pl.BlockSpec((tm, hv), lambda i: (i, col_r // hv)),
            pl.BlockSpec((tm, hv), lambda i: (i, 0)),
            pl.BlockSpec((tm, o_b.shape[1]), lambda i: (i, 0)),
            _resident(w_a.shape), _resident(w_b.shape), _resident(w_o.shape),
        ],
        out_specs=pl.BlockSpec((tm, d), lambda i: (i, 0)),
        out_shape=jax.ShapeDtypeStruct((n, d), F32),
        compiler_params=_cparams(("parallel",)),
        name="mix",
    )(x2, main, main, main, gla, o_b, w_a, w_b, w_o)


def _ffn_kernel(x_ref, g2_ref, wg_ref, wu_ref, wd_ref, gf_ref, o_ref, h_ref):
    j = pl.program_id(1)

    @pl.when(j == 0)
    def _():
        x = x_ref[...]
        ms = jnp.mean(x * x, axis=-1, keepdims=True)
        h_ref[...] = (x * lax.rsqrt(ms + NORM_EPS) * g2_ref[...]).astype(BF16)
        o_ref[...] = x

    h = h_ref[...]
    gate = _dot(h, wg_ref[...])
    up = _dot(h, wu_ref[...])
    act = (gate * _sigmoid(gate) * up).astype(BF16)
    o_ref[...] += _dot(act, wd_ref[...])

    @pl.when(j == pl.num_programs(1) - 1)
    def _():
        y = o_ref[...]
        ms = jnp.mean(y * y, axis=-1, keepdims=True)
        o_ref[...] = y * lax.rsqrt(ms + NORM_EPS) * gf_ref[...]


def _ffn(x1, g2, w_in, w_out, gf, *, tm, tf):
    n, d = x1.shape
    f = w_out.shape[0]
    nf = f // tf
    return pl.pallas_call(
        _ffn_kernel,
        grid=(n // tm, nf),
        in_specs=[
            pl.BlockSpec((tm, d), lambda i, j: (i, 0)),
            pl.BlockSpec((1, d), lambda i, j: (0, 0)),
            pl.BlockSpec((d, tf), lambda i, j: (0, j)),
            pl.BlockSpec((d, tf), lambda i, j: (0, j + nf)),
            pl.BlockSpec((tf, d), lambda i, j: (j, 0)),
            pl.BlockSpec((1, d), lambda i, j: (0, 0)),
        ],
        out_specs=pl.BlockSpec((tm, d), lambda i, j: (i, 0)),
        out_shape=jax.ShapeDtypeStruct((n, d), F32),
        scratch_shapes=[pltpu.VMEM((tm, d), BF16)],
        compiler_params=_cparams(("parallel", "arbitrary")),
        name="ffn",
    )(x1, g2, w_in, w_in, w_out, gf)


def _tile(n, pref):
    t = min(n, pref)
    assert n % t == 0, (n, t)
    return t


def _layer(x2, bsz, seq, norm1_g, w_in, w_a2, b_a2, gla_norm_g, lam_re, lam_im, log_dt,
           s5_b_re, s5_b_im, s5_c_re, s5_c_im, s5_d, w_glu, b_glu,
           w_branch_a, w_branch_b, w_out, norm2_g, w_ffn_in, w_ffn_out, final_g):
    n, d = x2.shape
    hk = w_a2.shape[1]
    hv = gla_norm_g.shape[0]
    rank = w_a2.shape[0]
    sw = w_glu.shape[0]
    dk, dv = hk // GLA_HEADS, hv // GLA_HEADS

    o_q, o_k, o_v, o_r = 0, hk, 2 * hk, 2 * hk + hv
    o_al = 2 * hk + 2 * hv
    o_u = o_al + rank
    o_ga = o_u + sw
    o_gb = o_ga + d
    assert o_gb + d == w_in.shape[1]
    tr = sw
    segments = ((o_ga, 2 * d), (o_v, 2 * hv), (o_q, 2 * hk), (o_u, sw))
    starts = [lo + k for lo, width in segments for k in range(0, width, tr)]
    w_re, w_low = _w_in_prep(jnp.swapaxes(w_in, 0, 1), starts=starts, low_lo=o_al, rank=rank, tr=tr)
    wa2p = jnp.pad(w_a2, ((0, LANES - rank), (0, 0)))
    col_v, col_r, col_q, col_k = 2 * d, 2 * d + hv, 2 * d + 2 * hv, 2 * d + 2 * hv + hk

    main, u, loga = _in_proj(x2, norm1_g.reshape(1, d), w_re, w_low, wa2p, b_a2.reshape(1, hk),
                             tm=_tile(n, 1024), tn=1024, sw=sw)

    ts = _tile(seq, 512)
    tc = _tile(ts, 256)
    blk = jnp.arange(tc) // GLA_CHUNK
    tri = ((blk[:, None] == blk[None, :]) & (jnp.arange(tc)[:, None] >= jnp.arange(tc)[None, :]))
    gla = _gla(main.reshape(bsz, seq, -1), loga.reshape(bsz, seq, hk), gla_norm_g.reshape(1, hv),
               tri.astype(BF16), ts=ts, tc=tc, col_q=col_q, col_k=col_k, col_v=col_v, dk=dk, dv=dv)

    m, wst, wout, pw = _s5_tables(lam_re, lam_im, log_dt, s5_b_re, s5_b_im, s5_c_re, s5_c_im)
    rows = seq // S5_STEPS
    o_b = _s5(u.reshape(bsz, rows, S5_STEPS * sw), m, wst, wout, pw, s5_d.reshape(1, sw),
              w_glu.astype(BF16), b_glu.reshape(1, sw), rb=_tile(rows, 256))

    x1 = _mix(x2, main, gla.reshape(n, hv), o_b.reshape(n, sw), w_branch_a.astype(BF16),
              w_branch_b.astype(BF16), w_out.astype(BF16), tm=_tile(n, 512), col_r=col_r)
    return _ffn(x1, norm2_g.reshape(1, d), w_ffn_in.astype(BF16), w_ffn_out.astype(BF16),
                final_g.reshape(1, d), tm=_tile(n, 1024), tf=512)


def kernel(x, norm1_g, w_in, w_a2, b_a2, gla_norm_g, lam_re, lam_im, log_dt, s5_b_re, s5_b_im,
           s5_c_re, s5_c_im, s5_d, w_glu, b_glu, w_branch_a, w_branch_b, w_out, norm2_g,
           w_ffn_in, w_ffn_out, final_norm_g):
    bsz, seq, d = x.shape
    assert norm1_g.shape[0] == 1, "single-layer block"
    out = _layer(x.reshape(bsz * seq, d), bsz, seq, norm1_g[0], w_in[0], w_a2[0], b_a2[0],
                 gla_norm_g[0], lam_re[0], lam_im[0], log_dt[0], s5_b_re[0], s5_b_im[0],
                 s5_c_re[0], s5_c_im[0], s5_d[0], w_glu[0], b_glu[0], w_branch_a[0],
                 w_branch_b[0], w_out[0], norm2_g[0], w_ffn_in[0], w_ffn_out[0], final_norm_g[0])
    return out.reshape(bsz, seq, d)
```

```python
import functools
import math

import jax
import jax.numpy as jnp
from jax import lax
from jax.experimental import pallas as pl
from jax.experimental.pallas import tpu as pltpu

F32 = jnp.float32
BF16 = jnp.bfloat16

NORM_EPS = 1e-6
GLA_HEADS = 4
GLA_CHUNK = 32
GLA_GATE_TAU = 16.0
GLA_SCORES_AHEAD = 2
LOG2_E = math.log2(math.e)
LANES = 128
SUBLANES = 8
MXU_TILE = 256
ROW_UNIT = 16
S5_STEPS = 8
S5_SLAB_GROUPS = 8
VMEM_LIMIT = 56 * 1024 * 1024


def _sigmoid(x):
    return 1.0 / (1.0 + jnp.exp(-x))


def _log_sigmoid(x):
    return jnp.minimum(x, 0.0) - jnp.log(1.0 + jnp.exp(-jnp.abs(x)))


def _gelu_tanh(x):
    c = math.sqrt(2.0 / math.pi)
    return 0.5 * x * (1.0 + jnp.tanh(c * (x + 0.044715 * (x * x * x))))


def _dot(a, b):
    return jnp.dot(a, b, preferred_element_type=F32)


def _cparams(sem):
    return pltpu.CompilerParams(dimension_semantics=sem, vmem_limit_bytes=VMEM_LIMIT)


def _resident(shape):
    zeros = (0,) * len(shape)
    return pl.BlockSpec(shape, lambda *_: zeros, pipeline_mode=pl.Buffered(1))


def _w_in_prep_kernel(starts_ref, w_ref, wlow_ref, out_ref, low_ref, *, rank):
    del starts_ref
    out_ref[...] = jnp.transpose(w_ref[...]).astype(BF16)

    @pl.when(pl.program_id(0) == 0)
    def _():
        lane = lax.broadcasted_iota(jnp.int32, low_ref.shape, 1)
        low_ref[...] = jnp.where(lane < rank, jnp.transpose(wlow_ref[...]), 0.0).astype(BF16)


def _w_in_prep(wt, *, starts, low_lo, rank, tr):
    _, d = wt.shape
    assert all(s % ROW_UNIT == 0 for s in starts) and low_lo % ROW_UNIT == 0
    n_blocks = len(starts)
    grid_spec = pltpu.PrefetchScalarGridSpec(
        num_scalar_prefetch=1,
        grid=(n_blocks,),
        in_specs=[
            pl.BlockSpec((pl.Element(tr), pl.Element(d)), lambda i, st: (st[i] * ROW_UNIT, 0)),
            pl.BlockSpec((pl.Element(LANES), pl.Element(d)), lambda i, st: (low_lo, 0)),
        ],
        out_specs=[
            pl.BlockSpec((d, tr), lambda i, st: (0, i)),
            pl.BlockSpec((d, LANES), lambda i, st: (0, 0)),
        ],
    )
    return pl.pallas_call(
        functools.partial(_w_in_prep_kernel, rank=rank),
        grid_spec=grid_spec,
        out_shape=[
            jax.ShapeDtypeStruct((d, n_blocks * tr), BF16),
            jax.ShapeDtypeStruct((d, LANES), BF16),
        ],
        compiler_params=_cparams(("arbitrary",)),
        name="w_in_prep",
    )(jnp.asarray([s // ROW_UNIT for s in starts], jnp.int32), wt, wt)


W_RING = 3


def _in_proj_kernel(x_ref, g_ref, w_hbm, wu_ref, wlow_ref, wa2_ref, ba2_ref,
                    main_ref, u_ref, loga_ref, h0_ref, h1_ref, uacc_ref, wbuf_ref, wsem,
                    *, n_main, n_uses):
    i = pl.program_id(0)
    j = pl.program_id(1)
    chunk = x_ref.shape[0]
    tn = wbuf_ref.shape[2]

    def w_copy(k):
        col = (k % n_main) * tn
        if not isinstance(k, int):
            col = pl.multiple_of(col, tn)
        slot = k % W_RING
        return pltpu.make_async_copy(w_hbm.at[:, pl.ds(col, tn)], wbuf_ref.at[slot], wsem.at[slot])

    def fetch_w():
        k = (i - 1) * n_main + j
        w_copy(k).wait()

        @pl.when(k + W_RING - 1 < n_uses)
        def _():
            w_copy(k + W_RING - 1).start()

        return k % W_RING

    def normalise_chunk(h_ref):
        x = x_ref[...]
        ms = jnp.mean(x * x, axis=-1, keepdims=True)
        rows = pl.ds(pl.multiple_of(j * chunk, chunk), chunk)
        h_ref[rows, :] = (x * lax.rsqrt(ms + NORM_EPS) * g_ref[...]).astype(BF16)

    def tail(h):
        a_low = _dot(h, wlow_ref[...])
        uacc = _dot(h, wu_ref[...])
        z = _dot(a_low, wa2_ref[...]) + ba2_ref[...]
        loga_ref[...] = _log_sigmoid(z) * (LOG2_E / GLA_GATE_TAU)
        rows, width = u_ref.shape[0], wu_ref.shape[1]
        for c in range(width // LANES):
            uacc_ref[c] = uacc[:, c * LANES:(c + 1) * LANES]
        for s in range(S5_STEPS):
            for c in range(width // LANES):
                lo = s * width + c * LANES
                u_ref[:, lo:lo + LANES] = uacc_ref[c, pl.ds(s, rows, stride=S5_STEPS), :].astype(BF16)

    @pl.when(i == 0)
    def _():
        @pl.when(j == 0)
        def _():
            for k in range(min(W_RING - 1, n_uses)):
                w_copy(k).start()

        normalise_chunk(h0_ref)

    for parity, (h_write, h_read) in enumerate(((h0_ref, h1_ref), (h1_ref, h0_ref))):
        active = (i > 0) & (i % 2 == parity)

        @pl.when(active & (j < n_main))
        def _(h_write=h_write, h_read=h_read):
            slot = fetch_w()
            normalise_chunk(h_write)
            main_ref[...] = _dot(h_read[...], wbuf_ref[slot]).astype(BF16)

        @pl.when(active & (j == n_main))
        def _(h_write=h_write, h_read=h_read):
            normalise_chunk(h_write)
            tail(h_read[...])


def _in_proj(x2, g, w, w_low, wa2p, ba2, *, tm, tn, sw):
    n, d = x2.shape
    n_main = (w.shape[1] - sw) // tn
    hk = wa2p.shape[1]
    nt = n // tm
    kern = functools.partial(_in_proj_kernel, n_main=n_main, n_uses=nt * n_main)
    n_steps = n_main + 1
    assert tm % n_steps == 0
    chunk = tm // n_steps

    def prev(i):
        return jnp.maximum(i - 1, 0)

    return pl.pallas_call(
        kern,
        grid=(nt + 1, n_steps),
        in_specs=[
            pl.BlockSpec((chunk, d), lambda i, j: (jnp.minimum(i, nt - 1) * n_steps + j, 0)),
            pl.BlockSpec((1, d), lambda i, j: (0, 0)),
            pl.BlockSpec(memory_space=pl.ANY),
            pl.BlockSpec((d, sw), lambda i, j: (0, n_main * tn // sw), pipeline_mode=pl.Buffered(1)),
            _resident(w_low.shape),
            _resident(wa2p.shape),
            _resident(ba2.shape),
        ],
        out_specs=[
            pl.BlockSpec((tm, tn), lambda i, j: (prev(i), jnp.where(i == 0, 0, jnp.minimum(j, n_main - 1)))),
            pl.BlockSpec((tm // S5_STEPS, S5_STEPS * sw), lambda i, j: (prev(i), 0)),
            pl.BlockSpec((tm, hk), lambda i, j: (prev(i), 0)),
        ],
        out_shape=[
            jax.ShapeDtypeStruct((n, n_main * tn), BF16),
            jax.ShapeDtypeStruct((n // S5_STEPS, S5_STEPS * sw), BF16),
            jax.ShapeDtypeStruct((n, hk), F32),
        ],
        scratch_shapes=[pltpu.VMEM((tm, d), BF16), pltpu.VMEM((tm, d), BF16),
                        pltpu.VMEM((sw // LANES, tm, LANES), F32),
                        pltpu.VMEM((W_RING, d, tn), BF16), pltpu.SemaphoreType.DMA((W_RING,))],
        compiler_params=_cparams(("arbitrary", "arbitrary")),
        name="in_proj",
    )(x2, g, w, w, w_low, wa2p, ba2)


def _kr_base(m):
    return GLA_CHUNK * (m * (m + 1) // 2)


def _gla_block(q_ref, k_ref, v_ref, loga_ref, ghn_ref, tri_ref,
               o_ref, s_ref, qd_ref, qb_ref, kb_ref, kr_ref, p_ref, *, n_sub, dk, dv):
    c = GLA_CHUNK
    shared = {}

    def decays():
        log_a = loga_ref[...]
        la_hi = log_a.astype(BF16)
        la_lo = (log_a - la_hi.astype(F32)).astype(BF16)
        tri = tri_ref[...]
        bc_all = _dot(tri, la_hi) + _dot(tri, la_lo)
        bl = [bc_all[(i + 1) * c - 1:(i + 1) * c, :] for i in range(n_sub)]
        bs = [jnp.zeros_like(bl[0])]
        for i in range(n_sub):
            bs.append(bs[-1] + bl[i])
        shared.update(bc_all=bc_all, bl=bl, bs=bs)

    def operands(i):
        bc_all, bl, bs = shared["bc_all"], shared["bl"], shared["bs"]
        btot = bs[n_sub]
        rows = slice(i * c, (i + 1) * c)
        b = bc_all[rows, :]
        q = q_ref[rows, :].astype(F32) * (dk ** -0.5)
        k = k_ref[rows, :].astype(F32)
        qd = q * jnp.exp2(b)
        ks = k * jnp.exp2(bl[i] - b)
        qd_ref[rows, :] = qd.astype(BF16)
        qb_ref[rows, :] = (qd * jnp.exp2(bs[i])).astype(BF16)
        kb_ref[rows, :] = (ks * jnp.exp2(btot - bs[i + 1])).astype(BF16)
        own = _kr_base(i) + i * c
        kr_ref[own:own + c, :] = (k * jnp.exp2(-b)).astype(BF16)
        for m in range(i + 1, n_sub):
            lo = _kr_base(m) + i * c
            kr_ref[lo:lo + c, :] = (ks * jnp.exp2(bs[m] - bs[i + 1])).astype(BF16)

    def scores(h):
        row = lax.broadcasted_iota(jnp.int32, (c, c), 0)
        col = lax.broadcasted_iota(jnp.int32, (c, c), 1)
        causal = row >= col
        ks_ = slice(h * dk, (h + 1) * dk)
        for m in range(n_sub):
            rows = slice(m * c, (m + 1) * c)
            keys = slice(_kr_base(m), _kr_base(m) + (m + 1) * c)
            s = lax.dot_general(qd_ref[rows, ks_], kr_ref[keys, ks_], (((1,), (1,)), ((), ())),
                                preferred_element_type=F32)
            if m > 0:
                p_ref[h, rows, 0:m * c] = s[:, 0:m * c].astype(BF16)
            p_ref[h, rows, m * c:(m + 1) * c] = jnp.where(causal, s[:, m * c:], 0.0).astype(BF16)

    def head(h):
        btot = shared["bs"][n_sub]
        ks_ = slice(h * dk, (h + 1) * dk)
        vs_ = slice(h * dv, (h + 1) * dv)
        v = v_ref[:, vs_]
        state = s_ref[h]
        shared["o", h] = _dot(p_ref[h], v) + _dot(qb_ref[:, ks_], state.astype(BF16))
        shared["state", h] = state

    def finish(h):
        btot = shared["bs"][n_sub]
        ks_ = slice(h * dk, (h + 1) * dk)
        vs_ = slice(h * dv, (h + 1) * dv)
        upd = lax.dot_general(kb_ref[:, ks_], v_ref[:, vs_], (((0,), (0,)), ((), ())),
                              preferred_element_type=F32)
        decay_col = jnp.transpose(jnp.broadcast_to(jnp.exp2(btot[:, ks_]), (dk, dk)))
        if dv > dk:
            decay_col = jnp.concatenate([decay_col] * (dv // dk), axis=1)
        s_ref[h] = decay_col * shared["state", h] + upd
        o = shared["o", h]
        o = o * lax.rsqrt(jnp.mean(o * o, axis=-1, keepdims=True) + NORM_EPS)
        o_ref[:, vs_] = (o * ghn_ref[:, vs_]).astype(BF16)

    stages = [functools.partial(scores, h) for h in range(GLA_SCORES_AHEAD)]
    for h in range(GLA_HEADS):
        if h + GLA_SCORES_AHEAD < GLA_HEADS:
            stages.append(functools.partial(scores, h + GLA_SCORES_AHEAD))
        stages.append(functools.partial(head, h))
        if h > 0:
            stages.append(functools.partial(finish, h - 1))
    stages.append(functools.partial(finish, GLA_HEADS - 1))
    return decays, [functools.partial(operands, i) for i in range(n_sub)], stages


def _gla_kernel(q_ref, k_ref, v_ref, loga_ref, ghn_ref, tri_ref,
                o_ref, s_ref, qd_ref, qb_ref, kb_ref, kr_ref, p_ref, *, tc, dk, dv):
    @pl.when(pl.program_id(1) == 0)
    def _():
        s_ref[...] = jnp.zeros_like(s_ref)
        p_ref[...] = jnp.zeros_like(p_ref)

    blocks = []
    for blk in range(q_ref.shape[0] // tc):
        rows = pl.ds(blk * tc, tc)
        blocks.append(_gla_block(
            q_ref.at[rows], k_ref.at[rows], v_ref.at[rows], loga_ref.at[rows], ghn_ref, tri_ref,
            o_ref.at[rows], s_ref, qd_ref.at[blk], qb_ref.at[blk], kb_ref.at[blk], kr_ref.at[blk],
            p_ref.at[blk], n_sub=tc // GLA_CHUNK, dk=dk, dv=dv))

    decays, operands, _ = blocks[0]
    decays()
    for stage in operands:
        stage()
    for blk, (_, _, heads) in enumerate(blocks):
        nxt = blocks[blk + 1] if blk + 1 < len(blocks) else None
        if nxt is not None:
            nxt[0]()
        per_stage = -(-len(nxt[1]) // len(heads)) if nxt is not None else 0
        for h, stage in enumerate(heads):
            stage()
            if nxt is not None:
                for operand_stage in nxt[1][h * per_stage:(h + 1) * per_stage]:
                    operand_stage()


def _gla(main3, loga3, ghn, tri, *, ts, tc, col_q, col_k, col_v, dk, dv):
    b, t, _ = main3.shape
    hk, hv = GLA_HEADS * dk, GLA_HEADS * dv
    kern = functools.partial(_gla_kernel, tc=tc, dk=dk, dv=dv)
    nb = ts // tc
    return pl.pallas_call(
        kern,
        grid=(b, t // ts),
        in_specs=[
            pl.BlockSpec((None, ts, hk), lambda i, j: (i, j, col_q // hk)),
            pl.BlockSpec((None, ts, hk), lambda i, j: (i, j, col_k // hk)),
            pl.BlockSpec((None, ts, hv), lambda i, j: (i, j, col_v // hv)),
            pl.BlockSpec((None, ts, hk), lambda i, j: (i, j, 0)),
            pl.BlockSpec((1, hv), lambda i, j: (0, 0)),
            pl.BlockSpec((tc, tc), lambda i, j: (0, 0)),
        ],
        out_specs=pl.BlockSpec((None, ts, hv), lambda i, j: (i, j, 0)),
        out_shape=jax.ShapeDtypeStruct((b, t, hv), BF16),
        scratch_shapes=[
            pltpu.VMEM((GLA_HEADS, dk, dv), F32),
            pltpu.VMEM((nb, tc, hk), BF16),
            pltpu.VMEM((nb, tc, hk), BF16),
            pltpu.VMEM((nb, tc, hk), BF16),
            pltpu.VMEM((nb, _kr_base(tc // GLA_CHUNK), hk), BF16),
            pltpu.VMEM((nb, GLA_HEADS, tc, tc), BF16),
        ],
        compiler_params=_cparams(("parallel", "arbitrary")),
        name="gla",
    )(main3, main3, main3, loga3, ghn, tri)


def _cmul(ar, ai, br, bi):
    return ar * br - ai * bi, ar * bi + ai * br


def _s5_tables_kernel(lr_r, li_r, ldt_r, lr_c, li_c, ldt_c, bre, bim, cre, cim,
                      m_ref, wst_ref, wout_ref, pw_ref, *, gc, p):
    L = S5_STEPS
    sw = S5_SLAB_GROUPS * gc
    sp = S5_SLAB_GROUPS * p

    def discretise(lr, li, ldt):
        dt = jnp.exp(ldt)
        mag = jnp.exp(lr * dt)
        return mag * jnp.cos(li * dt), mag * jnp.sin(li * dt)

    lr, li = lr_r[...], li_r[...]
    ar, ai = discretise(lr, li, ldt_r[...])
    den = lr * lr + li * li
    am1 = ar - 1.0
    f_re = (am1 * lr + ai * li) / den
    f_im = (ai * lr - am1 * li) / den

    rg = lax.broadcasted_iota(jnp.int32, (sw, sp), 0) // gc
    lg = lax.broadcasted_iota(jnp.int32, (sw, sp), 1) // p
    bmask = rg == lg
    br, bi = bre[...], bim[...]
    bbr = jnp.where(bmask, f_re * br - f_im * bi, 0.0)
    bbi = jnp.where(bmask, f_re * bi + f_im * br, 0.0)

    rg2 = lax.broadcasted_iota(jnp.int32, (sp, sw), 0) // p
    lg2 = lax.broadcasted_iota(jnp.int32, (sp, sw), 1) // gc
    cmask = rg2 == lg2
    cr = jnp.where(cmask, cre[...], 0.0)
    ci = jnp.where(cmask, cim[...], 0.0)

    pr, pi = [jnp.ones_like(ar)], [jnp.zeros_like(ai)]
    for _ in range(L):
        nr, ni = _cmul(pr[-1], pi[-1], ar, ai)
        pr.append(nr)
        pi.append(ni)

    def split(a):
        hi = a.astype(BF16)
        return hi, (a - hi.astype(F32)).astype(BF16)

    c_hi, c_lo = split(jnp.concatenate([cr, -ci], axis=0))

    kern = []
    for k in range(L):
        xr, xi = _cmul(bbr, bbi, pr[k], pi[k])
        j = L - 1 - k
        wst_ref[j * sw:(j + 1) * sw, 0:sp] = xr.astype(BF16)
        wst_ref[j * sw:(j + 1) * sw, sp:2 * sp] = xi.astype(BF16)
        x_hi, x_lo = split(jnp.concatenate([xr, xi], axis=1))
        kk = _dot(x_hi, c_hi) + (_dot(x_hi, c_lo) + _dot(x_lo, c_hi))
        kern.append(kk.astype(BF16))
    zero = jnp.zeros((sw, sw), BF16)
    for j in range(L):
        for i in range(L):
            m_ref[j * sw:(j + 1) * sw, i * sw:(i + 1) * sw] = kern[i - j] if i >= j else zero

    acr, aci = discretise(lr_c[...], li_c[...], ldt_c[...])
    qr, qi = acr, aci
    for i in range(L):
        zr, zi = _cmul(cr, ci, qr, qi)
        wout_ref[0:sp, i * sw:(i + 1) * sw] = zr.astype(BF16)
        wout_ref[sp:2 * sp, i * sw:(i + 1) * sw] = (-zi).astype(BF16)
        qr, qi = _cmul(qr, qi, acr, aci)

    a8r, a8i = pr[L], pi[L]
    tr, ti = a8r, a8i
    for i in range(SUBLANES):
        pw_ref[i:i + 1, 0:sp] = tr
        pw_ref[i:i + 1, sp:2 * sp] = ti
        tr, ti = _cmul(tr, ti, a8r, a8i)


def _s5_tables(lam_re, lam_im, log_dt, b_re, b_im, c_re, c_im):
    g, p = lam_re.shape
    gc = b_re.shape[2]
    ns = g // S5_SLAB_GROUPS
    sw, sp = S5_SLAB_GROUPS * gc, S5_SLAB_GROUPS * p
    lw = S5_STEPS * sw

    def rows(a):
        return a.reshape(ns, 1, sp)

    def cols(a):
        return a.reshape(ns, sp, 1)

    ldt = jnp.broadcast_to(log_dt[:, None], (g, p))

    def b_layout(b):
        bt = jnp.transpose(b, (0, 2, 1)).reshape(ns, sw, p)
        return jnp.tile(bt, (1, 1, S5_SLAB_GROUPS))

    def c_layout(c):
        ct = jnp.transpose(c.reshape(ns, S5_SLAB_GROUPS, gc, p), (0, 3, 1, 2)).reshape(ns, p, sw)
        return jnp.tile(ct, (1, S5_SLAB_GROUPS, 1))

    row_spec = pl.BlockSpec((None, 1, sp), lambda s: (s, 0, 0))
    col_spec = pl.BlockSpec((None, sp, 1), lambda s: (s, 0, 0))
    b_spec = pl.BlockSpec((None, sw, sp), lambda s: (s, 0, 0))
    c_spec = pl.BlockSpec((None, sp, sw), lambda s: (s, 0, 0))
    kern = functools.partial(_s5_tables_kernel, gc=gc, p=p)
    return pl.pallas_call(
        kern,
        grid=(ns,),
        in_specs=[row_spec, row_spec, row_spec, col_spec, col_spec, col_spec,
                  b_spec, b_spec, c_spec, c_spec],
        out_specs=[
            pl.BlockSpec((None, lw, lw), lambda s: (s, 0, 0)),
            pl.BlockSpec((None, lw, 2 * sp), lambda s: (s, 0, 0)),
            pl.BlockSpec((None, 2 * sp, lw), lambda s: (s, 0, 0)),
            pl.BlockSpec((None, SUBLANES, 2 * sp), lambda s: (s, 0, 0)),
        ],
        out_shape=[
            jax.ShapeDtypeStruct((ns, lw, lw), BF16),
            jax.ShapeDtypeStruct((ns, lw, 2 * sp), BF16),
            jax.ShapeDtypeStruct((ns, 2 * sp, lw), BF16),
            jax.ShapeDtypeStruct((ns, SUBLANES, 2 * sp), F32),
        ],
        compiler_params=_cparams(("parallel",)),
        name="s5_tables",
    )(rows(lam_re), rows(lam_im), rows(ldt), cols(lam_re), cols(lam_im), cols(ldt),
      b_layout(b_re), b_layout(b_im), c_layout(c_re), c_layout(c_im))


def _s5_kernel(u_ref, m_ref, wst_ref, wout_ref, pw_ref, d_ref, wglu_ref, bglu_ref,
               o_ref, carry_ref, sloc_ref, sprev_ref, y_ref, out_ref, *, n_slabs, width):
    L = S5_STEPS
    sw = LANES
    lw = L * sw
    rb = u_ref.shape[0]
    sp2 = sloc_ref.shape[2]
    sp = sp2 // 2

    @pl.when(pl.program_id(1) == 0)
    def _():
        carry_ref[...] = jnp.zeros_like(carry_ref)

    row_id = lax.broadcasted_iota(jnp.int32, (SUBLANES, sp2), 0)

    def slab_input(s):
        return jnp.concatenate(
            [u_ref[:, j * width + s * sw: j * width + (s + 1) * sw] for j in range(L)], axis=1)

    for s in range(n_slabs):
        sloc_ref[s] = _dot(slab_input(s), wst_ref[s])

    pws = [pw_ref[s] for s in range(n_slabs)]
    steps = [{sh: jnp.where(row_id >= sh, pws[s][sh - 1:sh, :], 0.0) for sh in (1, 2, 4)}
             for s in range(n_slabs)]
    carries = [carry_ref[s] for s in range(n_slabs)]

    def scan_group(slabs, gi):
        grp = slice(gi * SUBLANES, (gi + 1) * SUBLANES)
        for s in slabs:
            pwr, pwi = pws[s][:, :sp], pws[s][:, sp:]
            carry = carries[s]
            x = sloc_ref[s, grp, :]
            for sh in (1, 2, 4):
                sft = pltpu.roll(x, sh, axis=0)
                mult = steps[s][sh]
                tr, ti = _cmul(sft[:, :sp], sft[:, sp:], mult[:, :sp], mult[:, sp:])
                x = x + jnp.concatenate([tr, ti], axis=1)
            cr, ci = _cmul(carry[:, :sp], carry[:, sp:], pwr, pwi)
            incl = x + jnp.concatenate([cr, ci], axis=1)
            sprev_ref[s, grp, :] = jnp.where(row_id == 0, carry, pltpu.roll(incl, 1, axis=0))
            carries[s] = incl[SUBLANES - 1:SUBLANES, :]

    def y_tile(s, nt):
        hi = (nt + 1) * MXU_TILE
        cols = slice(nt * MXU_TILE, hi)
        y = (_dot(sprev_ref[s].astype(BF16), wout_ref[s, :, cols])
             + _dot(slab_input(s)[:, 0:hi], m_ref[s, 0:hi, cols]))
        for i in range(nt * MXU_TILE // sw, hi // sw):
            lo = i * sw - nt * MXU_TILE
            y_ref[:, i * width + s * sw: i * width + (s + 1) * sw] = y[:, lo:lo + sw]

    n_groups = rb // SUBLANES
    first, second = tuple(range(n_slabs // 2)), tuple(range(n_slabs // 2, n_slabs))
    for gi in range(n_groups):
        scan_group(first, gi)
    tiles = [(s, nt) for s in first for nt in range(lw // MXU_TILE)]
    every = n_groups // len(tiles)
    for gi in range(n_groups):
        scan_group(second, gi)
        if every and (gi + 1) % every == 0 and tiles:
            y_tile(*tiles.pop(0))
    for s, nt in tiles + [(s, nt) for s in second for nt in range(lw // MXU_TILE)]:
        y_tile(s, nt)
    for s in range(n_slabs):
        carry_ref[s] = carries[s]

    for i in range(L):
        cols = slice(i * width, (i + 1) * width)
        yi = y_ref[:, cols] + d_ref[...] * u_ref[:, cols].astype(F32)
        hh = _gelu_tanh(yi)
        gate = _sigmoid(_dot(hh.astype(BF16), wglu_ref[...]) + bglu_ref[...])
        val = hh * gate
        for c in range(width // LANES):
            out_ref[c, pl.ds(i, rb, stride=L), :] = val[:, c * LANES:(c + 1) * LANES]
    for c in range(width // LANES):
        o_ref[:, c * LANES:(c + 1) * LANES] = out_ref[c].astype(BF16)


def _s5(u3, m, wst, wout, pw, d_skip, wglu, bglu, *, rb):
    b, rows, lw = u3.shape
    ns = m.shape[0]
    width = lw // S5_STEPS
    sp2 = wst.shape[2]
    kern = functools.partial(_s5_kernel, n_slabs=ns, width=width)
    return pl.pallas_call(
        kern,
        grid=(b, rows // rb),
        in_specs=[
            pl.BlockSpec((None, rb, lw), lambda i, j: (i, j, 0)),
            _resident(m.shape), _resident(wst.shape), _resident(wout.shape), _resident(pw.shape),
            _resident(d_skip.shape), _resident(wglu.shape), _resident(bglu.shape),
        ],
        out_specs=pl.BlockSpec((None, rb * S5_STEPS, width), lambda i, j: (i, j, 0)),
        out_shape=jax.ShapeDtypeStruct((b, rows * S5_STEPS, width), BF16),
        scratch_shapes=[
            pltpu.VMEM((ns, 1, sp2), F32),
            pltpu.VMEM((ns, rb, sp2), F32),
            pltpu.VMEM((ns, rb, sp2), F32),
            pltpu.VMEM((rb, lw), F32),
            pltpu.VMEM((width // LANES, rb * S5_STEPS, LANES), F32),
        ],
        compiler_params=_cparams(("parallel", "arbitrary")),
        name="s5",
    )(u3, m, wst, wout, pw, d_skip, wglu, bglu)


def _mix_kernel(x_ref, ga_ref, gb_ref, r_ref, gla_ref, ob_ref, wa_ref, wb_ref, wo_ref, o_ref):
    r = r_ref[...].astype(F32)
    o_a = (r * _sigmoid(r) * gla_ref[...].astype(F32)).astype(BF16)
    a = _dot(o_a, wa_ref[...])
    b = _dot(ob_ref[...], wb_ref[...])
    mix = _sigmoid(ga_ref[...].astype(F32)) * a + _sigmoid(gb_ref[...].astype(F32)) * b
    o_ref[...] = x_ref[...] + _dot(mix.astype(BF16), wo_ref[...])


def _mix(x2, main, gla, o_b, w_a, w_b, w_o, *, tm, col_r):
    n, d = x2.shape
    hv = gla.shape[1]
    return pl.pallas_call(
        _mix_kernel,
        grid=(n // tm,),
        in_specs=[
            pl.BlockSpec((tm, d), lambda i: (i, 0)),
            pl.BlockSpec((tm, d), lambda i: (i, 0)),
            pl.BlockSpec((tm, d), lambda i: (i, 1)),
            pl.BlockSpec((tm, hv), lambda i: (i, col_r // hv)),
            pl.BlockSpec((tm, hv), lambda i: (i, 0)),
            pl.BlockSpec((tm, o_b.shape[1]), lambda i: (i, 0)),
            _resident(w_a.shape), _resident(w_b.shape), _resident(w_o.shape),
        ],
        out_specs=pl.BlockSpec((tm, d), lambda i: (i, 0)),
        out_shape=jax.ShapeDtypeStruct((n, d), F32),
        compiler_params=_cparams(("parallel",)),
        name="mix",
    )(x2, main, main, main, gla, o_b, w_a, w_b, w_o)


def _ffn_kernel(x_ref, g2_ref, wg_ref, wu_ref, wd_ref, gf_ref, o_ref, h_ref):
    j = pl.program_id(1)

    @pl.when(j == 0)
    def _():
        x = x_ref[...]
        ms = jnp.mean(x * x, axis=-1, keepdims=True)
        h_ref[...] = (x * lax.rsqrt(ms + NORM_EPS) * g2_ref[...]).astype(BF16)
        o_ref[...] = x

    h = h_ref[...]
    gate = _dot(h, wg_ref[...])
    up = _dot(h, wu_ref[...])
    act = (gate * _sigmoid(gate) * up).astype(BF16)
    o_ref[...] += _dot(act, wd_ref[...])

    @pl.when(j == pl.num_programs(1) - 1)
    def _():
        y = o_ref[...]
        ms = jnp.mean(y * y, axis=-1, keepdims=True)
        o_ref[...] = y * lax.rsqrt(ms + NORM_EPS) * gf_ref[...]


def _ffn(x1, g2, w_in, w_out, gf, *, tm, tf):
    n, d = x1.shape
    f = w_out.shape[0]
    nf = f // tf
    return pl.pallas_call(
        _ffn_kernel,
        grid=(n // tm, nf),
        in_specs=[
            pl.BlockSpec((tm, d), lambda i, j: (i, 0)),
            pl.BlockSpec((1, d), lambda i, j: (0, 0)),
            pl.BlockSpec((d, tf), lambda i, j: (0, j)),
            pl.BlockSpec((d, tf), lambda i, j: (0, j + nf)),
            pl.BlockSpec((tf, d), lambda i, j: (j, 0)),
            pl.BlockSpec((1, d), lambda i, j: (0, 0)),
        ],
        out_specs=pl.BlockSpec((tm, d), lambda i, j: (i, 0)),
        out_shape=jax.ShapeDtypeStruct((n, d), F32),
        scratch_shapes=[pltpu.VMEM((tm, d), BF16)],
        compiler_params=_cparams(("parallel", "arbitrary")),
        name="ffn",
    )(x1, g2, w_in, w_in, w_out, gf)


def _tile(n, pref):
    t = min(n, pref)
    assert n % t == 0, (n, t)
    return t


def _layer(x2, bsz, seq, norm1_g, w_in, w_a2, b_a2, gla_norm_g, lam_re, lam_im, log_dt,
           s5_b_re, s5_b_im, s5_c_re, s5_c_im, s5_d, w_glu, b_glu,
           w_branch_a, w_branch_b, w_out, norm2_g, w_ffn_in, w_ffn_out, final_g):
    n, d = x2.shape
    hk = w_a2.shape[1]
    hv = gla_norm_g.shape[0]
    rank = w_a2.shape[0]
    sw = w_glu.shape[0]
    dk, dv = hk // GLA_HEADS, hv // GLA_HEADS

    o_q, o_k, o_v, o_r = 0, hk, 2 * hk, 2 * hk + hv
    o_al = 2 * hk + 2 * hv
    o_u = o_al + rank
    o_ga = o_u + sw
    o_gb = o_ga + d
    assert o_gb + d == w_in.shape[1]
    tr = sw
    segments = ((o_ga, 2 * d), (o_v, 2 * hv), (o_q, 2 * hk), (o_u, sw))
    starts = [lo + k for lo, width in segments for k in range(0, width, tr)]
    w_re, w_low = _w_in_prep(jnp.swapaxes(w_in, 0, 1), starts=starts, low_lo=o_al, rank=rank, tr=tr)
    wa2p = jnp.pad(w_a2, ((0, LANES - rank), (0, 0)))
    col_v, col_r, col_q, col_k = 2 * d, 2 * d + hv, 2 * d + 2 * hv, 2 * d + 2 * hv + hk

    main, u, loga = _in_proj(x2, norm1_g.reshape(1, d), w_re, w_low, wa2p, b_a2.reshape(1, hk),
                             tm=_tile(n, 1024), tn=1024, sw=sw)

    ts = _tile(seq, 512)
    tc = _tile(ts, 256)
    blk = jnp.arange(tc) // GLA_CHUNK
    tri = ((blk[:, None] == blk[None, :]) & (jnp.arange(tc)[:, None] >= jnp.arange(tc)[None, :]))
    gla = _gla(main.reshape(bsz, seq, -1), loga.reshape(bsz, seq, hk), gla_norm_g.reshape(1, hv),
               tri.astype(BF16), ts=ts, tc=tc, col_q=col_q, col_k=col_k, col_v=col_v, dk=dk, dv=dv)

    m, wst, wout, pw = _s5_tables(lam_re, lam_im, log_dt, s5_b_re, s5_b_im, s5_c_re, s5_c_im)
    rows = seq // S5_STEPS
    o_b = _s5(u.reshape(bsz, rows, S5_STEPS * sw), m, wst, wout, pw, s5_d.reshape(1, sw),
              w_glu.astype(BF16), b_glu.reshape(1, sw), rb=_tile(rows, 256))

    x1 = _mix(x2, main, gla.reshape(n, hv), o_b.reshape(n, sw), w_branch_a.astype(BF16),
              w_branch_b.astype(BF16), w_out.astype(BF16), tm=_tile(n, 512), col_r=col_r)
    return _ffn(x1, norm2_g.reshape(1, d), w_ffn_in.astype(BF16), w_ffn_out.astype(BF16),
                final_g.reshape(1, d), tm=_tile(n, 1024), tf=512)


def kernel(x, norm1_g, w_in, w_a2, b_a2, gla_norm_g, lam_re, lam_im, log_dt, s5_b_re, s5_b_im,
           s5_c_re, s5_c_im, s5_d, w_glu, b_glu, w_branch_a, w_branch_b, w_out, norm2_g,
           w_ffn_in, w_ffn_out, final_norm_g):
    bsz, seq, d = x.shape
    assert norm1_g.shape[0] == 1, "single-layer block"
    out = _layer(x.reshape(bsz * seq, d), bsz, seq, norm1_g[0], w_in[0], w_a2[0], b_a2[0],
                 gla_norm_g[0], lam_re[0], lam_im[0], log_dt[0], s5_b_re[0], s5_b_im[0],
                 s5_c_re[0], s5_c_im[0], s5_d[0], w_glu[0], b_glu[0], w_branch_a[0],
                 w_branch_b[0], w_out[0], norm2_g[0], w_ffn_in[0], w_ffn_out[0], final_norm_g[0])
    return out.reshape(bsz, seq, d)
```
